```python
import math
import jax, jax.numpy as jnp
from jax import lax
import numpy as np

D_MODEL = 1024
BATCH = 1
SEQ = 16384
DEPTH = 2

HEAD_DIM = 64
ROPE_THETA = 10000.0
CONV_WIDTH = 512
DIFF_HEADS = 4
DIFF_V_DIM = 2 * HEAD_DIM
SWA_Q_HEADS = 8
SWA_KV_HEADS = 2
SWA_WINDOW = 128
SWA_BLOCK = 128
Q_BLOCK = 128
N_BRANCH = 3
BRANCH_WIDTH = 512
D_FF = 2816
LN_EPS = 1e-5
ALPHA = (2 * DEPTH) ** 0.25
BETA = (8 * DEPTH) ** -0.25

DIFF_QK = DIFF_HEADS * 2 * HEAD_DIM
DIFF_V = DIFF_HEADS * DIFF_V_DIM
SWA_Q = SWA_Q_HEADS * HEAD_DIM
SWA_KV = SWA_KV_HEADS * HEAD_DIM
SPLIT_SIZES = (CONV_WIDTH, CONV_WIDTH, CONV_WIDTH, DIFF_QK, DIFF_QK, DIFF_V, SWA_Q, SWA_KV, SWA_KV)
D_IN = 3 * CONV_WIDTH + 2 * DIFF_QK + DIFF_V + SWA_Q + 2 * SWA_KV
VALUE_SEGMENTS = (0, 5, 8)

kernel_name = "hybrid_gated_conv_diffattn_swa_encoder"


def layer_norm(x, g, b):
    xf = x.astype(jnp.float32)
    mu = jnp.mean(xf, axis=-1, keepdims=True)
    var = jnp.mean(jnp.square(xf - mu), axis=-1, keepdims=True)
    y = (xf - mu) * lax.rsqrt(var + LN_EPS)
    return (y * g.astype(jnp.float32) + b.astype(jnp.float32)).astype(x.dtype)


def rms_norm(x, g):
    xf = x.astype(jnp.float32)
    y = xf * lax.rsqrt(jnp.mean(jnp.square(xf), axis=-1, keepdims=True) + LN_EPS)
    return (y * g.astype(jnp.float32)).astype(x.dtype)


def dwconv3(x, w):
    xp = jnp.pad(x, ((0, 0), (1, 1), (0, 0)))
    return w[0] * xp[:, :-2] + w[1] * xp[:, 1:-1] + w[2] * xp[:, 2:]


def rope_tables(positions):
    inv_freq = 1.0 / (ROPE_THETA ** (jnp.arange(0, HEAD_DIM, 2, dtype=jnp.float32) / HEAD_DIM))
    ang = positions.astype(jnp.float32)[..., None] * inv_freq
    return jnp.cos(ang), jnp.sin(ang)


def apply_rope(x, cos, sin):
    shape = cos.shape[:2] + (1,) * (x.ndim - 3) + cos.shape[-1:]
    c = cos.reshape(shape).astype(x.dtype)
    s = sin.reshape(shape).astype(x.dtype)
    x1, x2 = jnp.split(x, 2, axis=-1)
    return jnp.concatenate([x1 * c - x2 * s, x2 * c + x1 * s], axis=-1)


def short_conv_mixer(xt, gate_b, gate_c, w):
    return gate_b * dwconv3(gate_c * xt, w)


def diff_attention(q, k, v, lam, subln_g, lambda_init):
    b, s, h = q.shape[:3]
    nb = s // Q_BLOCK
    scale = HEAD_DIM ** -0.5
    qb = q.reshape(b, nb, Q_BLOCK, h, 2, HEAD_DIM).transpose(1, 0, 2, 3, 4, 5)

    def block(qi):
        sc = jnp.einsum('bqhjd,bkhjd->bhjqk', qi, k).astype(jnp.float32) * scale
        p = jax.nn.softmax(sc, axis=-1)
        a = p[:, :, 0] - lam * p[:, :, 1]
        return jnp.einsum('bhqk,bkhe->bqhe', a.astype(v.dtype), v)

    o = lax.map(block, qb)
    o = o.transpose(1, 0, 2, 3, 4).reshape(b, s, h, DIFF_V_DIM)
    o = rms_norm(o, subln_g) * (1.0 - lambda_init)
    return o.reshape(b, s, h * DIFF_V_DIM)


def window_attention(q, k, v, sink):
    b, s = q.shape[:2]
    nb = s // SWA_BLOCK
    g = SWA_Q_HEADS // SWA_KV_HEADS
    qb = q.reshape(b, nb, SWA_BLOCK, SWA_KV_HEADS, g, HEAD_DIM)

    def band(t):
        tp = jnp.pad(t, ((0, 0), (SWA_BLOCK, SWA_BLOCK), (0, 0), (0, 0)))
        tp = tp.reshape(b, nb + 2, SWA_BLOCK, SWA_KV_HEADS, HEAD_DIM)
        return jnp.concatenate([tp[:, :-2], tp[:, 1:-1], tp[:, 2:]], axis=2)

    kb, vb = band(k), band(v)
    sc = jnp.einsum('bnqhgd,bnkhd->bnhgqk', qb, kb).astype(jnp.float32) * (HEAD_DIM ** -0.5)
    qi = jnp.arange(SWA_BLOCK)[:, None]
    kj = jnp.arange(3 * SWA_BLOCK)[None, :]
    in_window = jnp.abs(kj - SWA_BLOCK - qi) <= SWA_WINDOW
    key_pos = (jnp.arange(nb) * SWA_BLOCK)[:, None, None] - SWA_BLOCK + kj[None]
    valid = in_window[None] & (key_pos >= 0) & (key_pos < s)
    sc = jnp.where(valid[None, :, None, None], sc, -jnp.inf)
    sink_l = sink.astype(jnp.float32).reshape(SWA_KV_HEADS, g)[None, None, :, :, None, None]
    m = jnp.maximum(jnp.max(sc, axis=-1, keepdims=True), sink_l)
    e = jnp.exp(sc - m)
    p = e / (jnp.sum(e, axis=-1, keepdims=True) + jnp.exp(sink_l - m))
    o = jnp.einsum('bnhgqk,bnkhd->bnqhgd', p.astype(v.dtype), vb)
    return o.reshape(b, s, SWA_Q_HEADS * HEAD_DIM)


def conv_ffn(x, w_up, conv_w, w_down):
    hdn = dwconv3(x @ w_up, conv_w)
    gate, up = jnp.split(hdn, 2, axis=-1)
    return (jax.nn.gelu(gate, approximate=False) * up) @ w_down


def setup_inputs(seed: int = 0) -> dict:
    key = jax.random.key(seed)
    ks = jax.random.split(key, 24)
    f32 = jnp.float32
    nrm = lambda k, shape, sc: jax.random.normal(k, shape, f32) * sc

    col_scale = np.concatenate([np.full((n,), BETA if i in VALUE_SEGMENTS else 1.0, np.float32)
                                for i, n in enumerate(SPLIT_SIZES)])
    x = jax.random.normal(ks[0], (BATCH, SEQ, D_MODEL), f32)
    positions = jnp.broadcast_to(jnp.arange(SEQ, dtype=jnp.int32), (BATCH, SEQ))
    return {
        "x": x,
        "positions": positions,
        "ln_in_g": 1.0 + nrm(ks[1], (D_MODEL,), 0.02),
        "ln_in_b": nrm(ks[2], (D_MODEL,), 0.02),
        "w_in": nrm(ks[3], (DEPTH, D_MODEL, D_IN), D_MODEL ** -0.5) * jnp.asarray(col_scale),
        "conv_w": nrm(ks[4], (DEPTH, 3, CONV_WIDTH), 3 ** -0.5),
        "diff_lambda": nrm(ks[5], (DEPTH, 4, HEAD_DIM), 0.1),
        "diff_subln_g": 1.0 + nrm(ks[6], (DEPTH, DIFF_V_DIM), 0.02),
        "swa_sink": nrm(ks[7], (DEPTH, SWA_Q_HEADS), 0.5),
        "w_branch_gate": nrm(ks[8], (DEPTH, D_MODEL, N_BRANCH * D_MODEL), D_MODEL ** -0.5),
        "b_branch_gate": nrm(ks[9], (DEPTH, N_BRANCH * D_MODEL), 0.02),
        "w_branch": nrm(ks[10], (DEPTH, N_BRANCH, BRANCH_WIDTH, D_MODEL), BRANCH_WIDTH ** -0.5 * BETA),
        "w_o": nrm(ks[11], (DEPTH, D_MODEL, D_MODEL), D_MODEL ** -0.5 * BETA),
        "ln_mix_g": 1.0 + nrm(ks[12], (DEPTH, D_MODEL), 0.02),
        "ln_mix_b": nrm(ks[13], (DEPTH, D_MODEL), 0.02),
        "w_ffn_up": nrm(ks[14], (DEPTH, D_MODEL, 2 * D_FF), D_MODEL ** -0.5 * BETA),
        "ffn_conv_w": nrm(ks[15], (DEPTH, 3, 2 * D_FF), 3 ** -0.5),
        "w_ffn_down": nrm(ks[16], (DEPTH, D_FF, D_MODEL), D_FF ** -0.5 * BETA),
        "ln_ffn_g": 1.0 + nrm(ks[17], (DEPTH, D_MODEL), 0.02),
        "ln_ffn_b": nrm(ks[18], (DEPTH, D_MODEL), 0.02),
    }


def reference(x, positions, ln_in_g, ln_in_b, w_in, conv_w, diff_lambda, diff_subln_g, swa_sink,
              w_branch_gate, b_branch_gate, w_branch, w_o, ln_mix_g, ln_mix_b,
              w_ffn_up, ffn_conv_w, w_ffn_down, ln_ffn_g, ln_ffn_b):
    b, s, _ = x.shape
    split_at = np.cumsum(SPLIT_SIZES)[:-1].tolist()
    cos, sin = rope_tables(positions)
    h = layer_norm(x, ln_in_g, ln_in_b)
    for l in range(DEPTH):
        lambda_init = 0.8 - 0.6 * math.exp(-0.3 * l)
        proj = h @ w_in[l]
        a_x, a_b, a_c, dq, dk, dv, sq, sk, sv = jnp.split(proj, split_at, axis=-1)

        y_a = short_conv_mixer(a_x, a_b, a_c, conv_w[l])

        dq = apply_rope(dq.reshape(b, s, DIFF_HEADS, 2, HEAD_DIM), cos, sin)
        dk = apply_rope(dk.reshape(b, s, DIFF_HEADS, 2, HEAD_DIM), cos, sin)
        dv = dv.reshape(b, s, DIFF_HEADS, DIFF_V_DIM)
        lp = diff_lambda[l].astype(jnp.float32)
        lam = jnp.exp(jnp.sum(lp[0] * lp[1])) - jnp.exp(jnp.sum(lp[2] * lp[3])) + lambda_init
        y_b = diff_attention(dq, dk, dv, lam, diff_subln_g[l], lambda_init)

        sq = apply_rope(sq.reshape(b, s, SWA_Q_HEADS, HEAD_DIM), cos, sin)
        sk = apply_rope(sk.reshape(b, s, SWA_KV_HEADS, HEAD_DIM), cos, sin)
        sv = sv.reshape(b, s, SWA_KV_HEADS, HEAD_DIM)
        y_c = window_attention(sq, sk, sv, swa_sink[l])

        ys = jnp.stack([y_a, y_b, y_c], axis=2)
        branches = jnp.einsum('bsnc,ncd->bsnd', ys, w_branch[l])
        gates = jax.nn.sigmoid(h @ w_branch_gate[l] + b_branch_gate[l]).reshape(b, s, N_BRANCH, D_MODEL)
        mix = jnp.sum(gates * branches, axis=2) @ w_o[l]
        h = layer_norm(ALPHA * h + mix, ln_mix_g[l], ln_mix_b[l])

        f = conv_ffn(h, w_ffn_up[l], ffn_conv_w[l], w_ffn_down[l])
        h = layer_norm(ALPHA * h + f, ln_ffn_g[l], ln_ffn_b[l])
    return h
```

```python
import functools
import math

import jax
import jax.numpy as jnp
from jax import lax
from jax.experimental import pallas as pl
from jax.experimental.pallas import tpu as pltpu

F32 = jnp.float32
BF16 = jnp.bfloat16

D_MODEL = 1024
HEAD_DIM = 64
HALF = HEAD_DIM // 2
ROPE_THETA = 10000.0
CONV_WIDTH = 512
DIFF_HEADS = 4
DIFF_V_DIM = 2 * HEAD_DIM
SWA_Q_HEADS = 8
SWA_KV_HEADS = 2
SWA_WINDOW = 128
N_BRANCH = 3
BRANCH_WIDTH = 512
D_FF = 2816
LN_EPS = 1e-5
LANES = 128
SUBLANES = 8

DIFF_QK = DIFF_HEADS * 2 * HEAD_DIM
DIFF_V = DIFF_HEADS * DIFF_V_DIM
SWA_Q = SWA_Q_HEADS * HEAD_DIM
SWA_KV = SWA_KV_HEADS * HEAD_DIM
N_NAT = 3 * CONV_WIDTH + DIFF_QK + SWA_KV
N_TR = DIFF_QK + DIFF_V + SWA_Q + SWA_KV

ROW_TILE = 512
Q_TILE = 256
FFN_CHUNK = 256
VMEM_LIMIT = 56 * 1024 * 1024
NEG_BIG = -1e30

LOG2E = math.log2(math.e)
Q_SCALE = HEAD_DIM ** -0.5 * LOG2E


def _cparams(sem):
    return pltpu.CompilerParams(dimension_semantics=sem, vmem_limit_bytes=VMEM_LIMIT)


def _const_spec(shape):
    nd = len(shape)
    return pl.BlockSpec(shape, lambda *_: (0,) * nd, pipeline_mode=pl.Buffered(1))


def _layer_norm(x, g, b):
    mu = jnp.mean(x, axis=-1, keepdims=True)
    xc = x - mu
    var = jnp.mean(xc * xc, axis=-1, keepdims=True)
    return xc * lax.rsqrt(var + LN_EPS) * g + b


def _rope_table_kernel(pos_col_ref, pos_row_ref, inv_row_ref, inv_col_ref,
                       cos_n_ref, sin_n_ref, cos_t_ref, sin_t_ref):
    ang_n = pos_col_ref[...].astype(F32) * inv_row_ref[...]
    lane = lax.broadcasted_iota(jnp.int32, ang_n.shape, 1)
    cos_n_ref[...] = jnp.cos(ang_n)
    sin_n_ref[...] = jnp.where(lane < 2 * HALF, -jnp.sin(ang_n), jnp.sin(ang_n))
    ang_t = inv_col_ref[...] * pos_row_ref[...].astype(F32)
    row = lax.broadcasted_iota(jnp.int32, ang_t.shape, 0)
    cos_t_ref[...] = jnp.cos(ang_t)
    sin_t_ref[...] = jnp.where(row < 2 * HALF, -jnp.sin(ang_t), jnp.sin(ang_t))


def _rope_tables(positions, seq):
    ts = ROW_TILE
    inv_freq = 1.0 / (ROPE_THETA ** (jnp.arange(0, HEAD_DIM, 2, dtype=F32) / HEAD_DIM))
    inv_row = jnp.tile(inv_freq, LANES // HALF).reshape(1, LANES)
    inv_col = inv_row.reshape(LANES, 1)
    pos_col = positions.reshape(seq, 1)
    pos_row = positions.reshape(1, seq)
    return pl.pallas_call(
        _rope_table_kernel,
        grid=(seq // ts,),
        in_specs=[pl.BlockSpec((ts, 1), lambda i: (i, 0)),
                  pl.BlockSpec((1, ts), lambda i: (0, i)),
                  _const_spec((1, LANES)), _const_spec((LANES, 1))],
        out_specs=[pl.BlockSpec((ts, LANES), lambda i: (i, 0)),
                   pl.BlockSpec((ts, LANES), lambda i: (i, 0)),
                   pl.BlockSpec((LANES, ts), lambda i: (0, i)),
                   pl.BlockSpec((LANES, ts), lambda i: (0, i))],
        out_shape=[jax.ShapeDtypeStruct((seq, LANES), F32),
                   jax.ShapeDtypeStruct((seq, LANES), F32),
                   jax.ShapeDtypeStruct((LANES, seq), F32),
                   jax.ShapeDtypeStruct((LANES, seq), F32)],
        compiler_params=_cparams(("arbitrary",)),
        name="rope_tables",
    )(pos_col, pos_row, inv_row, inv_col)


def _rope_rows(x, cos_n, sin_n):
    return x * cos_n + pltpu.roll(x, 2 * HALF, axis=1) * sin_n


def _rope_cols(x, cos_t, sin_t):
    swapped = jnp.concatenate([x[2 * HALF:], x[:2 * HALF]], axis=0)
    return x * cos_t + swapped * sin_t


def _proj_kernel(*refs, apply_ln):
    if apply_ln:
        x_ref, g_ref, b_ref = refs[:3]
        refs = refs[3:]
        h = _layer_norm(x_ref[...], g_ref[...], b_ref[...])
    else:
        x_ref = refs[0]
        refs = refs[1:]
        h = x_ref[...]
    (wn_ref, wt_ref, cos_n_ref, sin_n_ref, cos_t_ref, sin_t_ref) = refs[:6]
    outs = refs[6:]
    if apply_ln:
        h_out_ref = outs[0]
        outs = outs[1:]
        h_out_ref[...] = h
    u_ref, ab_ref, dk_ref, sk_ref, dqt_ref, dvt_ref, sqt_ref, svt_ref = outs

    hb = h.astype(BF16)
    cos_n, sin_n = cos_n_ref[...], sin_n_ref[...]
    cos_t, sin_t = cos_t_ref[...], sin_t_ref[...]

    def nat(lo, hi):
        return jnp.dot(hb, wn_ref[:, lo:hi], preferred_element_type=F32)

    tr_all = lax.dot_general(wt_ref[...], hb, (((1,), (1,)), ((), ())),
                             preferred_element_type=F32)

    def tr(lo, hi):
        return tr_all[lo:hi]

    cw = CONV_WIDTH
    a_x = nat(0, cw)
    a_c = nat(2 * cw, 3 * cw)
    u_ref[...] = (a_c * a_x).astype(BF16)
    ab_ref[...] = nat(cw, 2 * cw).astype(BF16)

    base = 3 * cw
    for hh in range(DIFF_HEADS):
        xk = nat(base + hh * LANES, base + (hh + 1) * LANES)
        dk_ref[hh] = _rope_rows(xk, cos_n, sin_n).astype(BF16)
    base += DIFF_QK
    sk_ref[...] = _rope_rows(nat(base, base + SWA_KV), cos_n, sin_n).astype(BF16)

    for hh in range(DIFF_HEADS):
        xq = tr(hh * LANES, (hh + 1) * LANES)
        dqt_ref[hh] = (_rope_cols(xq, cos_t, sin_t) * Q_SCALE).astype(BF16)
    base = DIFF_QK
    for hh in range(DIFF_HEADS):
        dvt_ref[hh, 0] = tr(base + hh * LANES, base + (hh + 1) * LANES).astype(BF16)
    base += DIFF_V
    for t in range(SWA_Q // LANES):
        xq = tr(base + t * LANES, base + (t + 1) * LANES)
        sqt_ref[t] = (_rope_cols(xq, cos_t, sin_t) * Q_SCALE).astype(BF16)
    base += SWA_Q
    svt_ref[...] = tr(base, base + SWA_KV).astype(BF16)


def _project(h_or_x, ln_gb, w_nat, w_tr, tables, seq):
    tm = ROW_TILE
    nt = seq // tm
    apply_ln = ln_gb is not None
    cos_n, sin_n, cos_t, sin_t = tables
    row_spec = lambda w: pl.BlockSpec((tm, w), lambda i: (i, 0))
    in_specs = [row_spec(D_MODEL)]
    args = [h_or_x]
    if apply_ln:
        in_specs += [_const_spec((1, D_MODEL)), _const_spec((1, D_MODEL))]
        args += [ln_gb[0].reshape(1, D_MODEL), ln_gb[1].reshape(1, D_MODEL)]
    in_specs += [_const_spec((D_MODEL, N_NAT)), _const_spec((N_TR, D_MODEL)),
                 row_spec(LANES), row_spec(LANES),
                 pl.BlockSpec((LANES, tm), lambda i: (0, i)),
                 pl.BlockSpec((LANES, tm), lambda i: (0, i))]
    args += [w_nat, w_tr, cos_n, sin_n, cos_t, sin_t]
    out_specs, out_shape = [], []
    if apply_ln:
        out_specs.append(row_spec(D_MODEL))
        out_shape.append(jax.ShapeDtypeStruct((seq, D_MODEL), F32))
    out_specs += [
        row_spec(CONV_WIDTH), row_spec(CONV_WIDTH),
        pl.BlockSpec((DIFF_HEADS, tm, LANES), lambda i: (0, i, 0)),
        row_spec(LANES),
        pl.BlockSpec((DIFF_HEADS, LANES, tm), lambda i: (0, 0, i)),
        pl.BlockSpec((DIFF_HEADS, 1, LANES, tm), lambda i: (0, i, 0, 0)),
        pl.BlockSpec((SWA_Q // LANES, LANES, tm), lambda i: (0, 0, i)),
        pl.BlockSpec((LANES, tm), lambda i: (0, i)),
    ]
    out_shape += [
        jax.ShapeDtypeStruct((seq, CONV_WIDTH), BF16),
        jax.ShapeDtypeStruct((seq, CONV_WIDTH), BF16),
        jax.ShapeDtypeStruct((DIFF_HEADS, seq, LANES), BF16),
        jax.ShapeDtypeStruct((seq, LANES), BF16),
        jax.ShapeDtypeStruct((DIFF_HEADS, LANES, seq), BF16),
        jax.ShapeDtypeStruct((DIFF_HEADS, nt, LANES, tm), BF16),
        jax.ShapeDtypeStruct((SWA_Q // LANES, LANES, seq), BF16),
        jax.ShapeDtypeStruct((LANES, seq), BF16),
    ]
    return pl.pallas_call(
        functools.partial(_proj_kernel, apply_ln=apply_ln),
        grid=(nt,), in_specs=in_specs, out_specs=out_specs, out_shape=out_shape,
        compiler_params=_cparams(("arbitrary",)),
        name="proj_ln" if apply_ln else "proj",
    )(*args)


def _pair_masks(qt):
    row = lax.broadcasted_iota(jnp.int32, qt.shape, 0)
    is_a = (row & HALF) == 0
    zero = jnp.zeros_like(qt)
    return jnp.where(is_a, qt, zero), jnp.where(is_a, zero, qt)


def _diff_attn_kernel(lam_ref, g_ref, qt_ref, k_ref, vt_ref, o_ref, acc_a, acc_b,
                      *, lambda_init, tk, n_kv):
    qa, qb = _pair_masks(qt_ref[...])
    tq = qa.shape[1]
    acc_a[...] = jnp.zeros_like(acc_a)
    acc_b[...] = jnp.zeros_like(acc_b)

    def softmax_step(kb, vb, qx, m, l, acc):
        s = jnp.dot(kb, qx, preferred_element_type=F32)
        m_new = jnp.maximum(m, jnp.max(s, axis=0, keepdims=True))
        alpha = jnp.exp2(m - m_new)
        p = jnp.exp2(s - m_new)
        l_new = alpha * l + jnp.sum(p, axis=0, keepdims=True)
        acc[...] = alpha * acc[...] + jnp.dot(vb, p.astype(BF16), preferred_element_type=F32)
        return m_new, l_new

    def body(j, carry):
        ma, la, mb, lb = carry
        off = pl.multiple_of(j * tk, tk)
        kb = k_ref[pl.ds(off, tk), :]
        vb = vt_ref[j]
        ma, la = softmax_step(kb, vb, qa, ma, la, acc_a)
        mb, lb = softmax_step(kb, vb, qb, mb, lb, acc_b)
        return ma, la, mb, lb

    m0 = jnp.full((1, tq), NEG_BIG, F32)
    l0 = jnp.zeros((1, tq), F32)
    _, la, _, lb = lax.fori_loop(0, n_kv, body, (m0, l0, m0, l0))

    lp = lam_ref[...]
    s01 = jnp.sum(lp[0:1] * lp[1:2], axis=1, keepdims=True)
    s23 = jnp.sum(lp[2:3] * lp[3:4], axis=1, keepdims=True)
    lam = jnp.exp(s01) - jnp.exp(s23) + lambda_init
    o = acc_a[...] * (1.0 / la) - lam * (acc_b[...] * (1.0 / lb))
    ms = jnp.mean(o * o, axis=0, keepdims=True)
    y = o * lax.rsqrt(ms + LN_EPS) * g_ref[...] * (1.0 - lambda_init)
    o_ref[...] = y.T.astype(BF16)


def _diff_attention(lam_p, subln_g, dqt, dk, dvt, lambda_init, seq):
    tq = Q_TILE
    n_kv, tk = dvt.shape[1], dvt.shape[3]
    return pl.pallas_call(
        functools.partial(_diff_attn_kernel, lambda_init=lambda_init, tk=tk, n_kv=n_kv),
        grid=(DIFF_HEADS, seq // tq),
        in_specs=[_const_spec(lam_p.shape), _const_spec((DIFF_V_DIM, 1)),
                  pl.BlockSpec((None, LANES, tq), lambda h, i: (h, 0, i)),
                  pl.BlockSpec((None, seq, LANES), lambda h, i: (h, 0, 0)),
                  pl.BlockSpec((None, n_kv, LANES, tk), lambda h, i: (h, 0, 0, 0))],
        out_specs=pl.BlockSpec((tq, DIFF_V_DIM), lambda h, i: (i, h)),
        out_shape=jax.ShapeDtypeStruct((seq, DIFF_V), BF16),
        scratch_shapes=[pltpu.VMEM((DIFF_V_DIM, tq), F32), pltpu.VMEM((DIFF_V_DIM, tq), F32)],
        compiler_params=_cparams(("arbitrary", "arbitrary")),
        name="diff_attn",
    )(lam_p, subln_g.reshape(DIFF_V_DIM, 1), dqt, dk, dvt)


def _window_attn_kernel(sink_ref, qt_ref, kp_ref, km_ref, kn_ref, vp_ref, vm_ref, vn_ref, o_ref,
                        *, n_blocks):
    i = pl.program_id(0)
    tq = qt_ref.shape[2]
    w = SWA_WINDOW
    kcat = jnp.concatenate([kp_ref[...], km_ref[...], kn_ref[...]], axis=0)
    vcat = jnp.concatenate([vp_ref[...], vm_ref[...], vn_ref[...]], axis=1)
    nk = tq + 2 * w
    r = lax.broadcasted_iota(jnp.int32, (nk, tq), 0)
    c = lax.broadcasted_iota(jnp.int32, (nk, tq), 1)
    r_lo = jnp.where(i > 0, 0, w)
    r_hi = jnp.where(i < n_blocks - 1, nk, tq + w)
    valid = (r >= jnp.maximum(c, r_lo)) & (r <= c + 2 * w) & (r < r_hi)
    sink = sink_ref[...] * LOG2E
    outs = [None] * SWA_Q_HEADS
    group = SWA_Q_HEADS // SWA_KV_HEADS
    for t in range(group):
        qa, qb = _pair_masks(qt_ref[t])
        for c_kv, qx in ((0, qa), (1, qb)):
            head = c_kv * group + t
            s = jnp.dot(kcat, qx, preferred_element_type=F32)
            s = jnp.where(valid, s, NEG_BIG)
            sk = sink[:, head:head + 1]
            m = jnp.maximum(jnp.max(s, axis=0, keepdims=True), sk)
            p = jnp.exp2(s - m)
            denom = jnp.sum(p, axis=0, keepdims=True) + jnp.exp2(sk - m)
            vt = vcat[c_kv * HEAD_DIM:(c_kv + 1) * HEAD_DIM, :]
            ot = jnp.dot(vt, p.astype(BF16), preferred_element_type=F32)
            outs[head] = ot * (1.0 / denom)
    o_ref[...] = jnp.concatenate(outs, axis=0).T.astype(BF16)


def _window_attention(sink, sqt, sk, svt, seq):
    tq = Q_TILE
    w = SWA_WINDOW
    nb = seq // tq
    per = tq // w
    last = seq // w - 1
    prev_idx = lambda i: jnp.maximum(i * per - 1, 0)
    next_idx = lambda i: jnp.minimum((i + 1) * per, last)
    return pl.pallas_call(
        functools.partial(_window_attn_kernel, n_blocks=nb),
        grid=(nb,),
        in_specs=[_const_spec((1, SWA_Q_HEADS)),
                  pl.BlockSpec((SWA_Q // LANES, LANES, tq), lambda i: (0, 0, i)),
                  pl.BlockSpec((w, LANES), lambda i: (prev_idx(i), 0)),
                  pl.BlockSpec((tq, LANES), lambda i: (i, 0)),
                  pl.BlockSpec((w, LANES), lambda i: (next_idx(i), 0)),
                  pl.BlockSpec((LANES, w), lambda i: (0, prev_idx(i))),
                  pl.BlockSpec((LANES, tq), lambda i: (0, i)),
                  pl.BlockSpec((LANES, w), lambda i: (0, next_idx(i)))],
        out_specs=pl.BlockSpec((tq, SWA_Q), lambda i: (i, 0)),
        out_shape=jax.ShapeDtypeStruct((seq, SWA_Q), BF16),
        compiler_params=_cparams(("arbitrary",)),
        name="window_attn",
    )(sink.reshape(1, SWA_Q_HEADS), sqt, sk, sk, sk, svt, svt, svt)


def _shift_rows(x, prev_row, next_row):
    n = x.shape[0]
    row = lax.broadcasted_iota(jnp.int32, x.shape, 0)
    up = jnp.where(row == 0, prev_row, pltpu.roll(x, 1, axis=0))
    down = jnp.where(row == n - 1, next_row, pltpu.roll(x, n - 1, axis=0))
    return up, down


def _merge_kernel(h_ref, u_ref, up_ref, un_ref, ab_ref, yb_ref, yc_ref, cw_ref,
                  wg_ref, bg_ref, wb_ref, wo_ref, g_ref, b_ref, o_ref, *, n_tiles, alpha):
    i = pl.program_id(0)
    h = h_ref[...]
    hb = h.astype(BF16)
    u = u_ref[...].astype(F32)
    prev_row = jnp.where(i > 0, up_ref[SUBLANES - 1:SUBLANES, :].astype(F32), 0.0)
    next_row = jnp.where(i < n_tiles - 1, un_ref[0:1, :].astype(F32), 0.0)
    u_up, u_dn = _shift_rows(u, prev_row, next_row)
    cw = cw_ref[...]
    y_a = ab_ref[...].astype(F32) * (cw[0:1] * u_up + cw[1:2] * u + cw[2:3] * u_dn)
    ys = (y_a.astype(BF16), yb_ref[...], yc_ref[...])
    mix = None
    for n in range(N_BRANCH):
        lo, hi = n * D_MODEL, (n + 1) * D_MODEL
        gate = jax.nn.sigmoid(jnp.dot(hb, wg_ref[:, lo:hi], preferred_element_type=F32)
                              + bg_ref[:, lo:hi])
        term = gate * jnp.dot(ys[n], wb_ref[n], preferred_element_type=F32)
        mix = term if mix is None else mix + term
    out = jnp.dot(mix.astype(BF16), wo_ref[...], preferred_element_type=F32)
    o_ref[...] = _layer_norm(alpha * h + out, g_ref[...], b_ref[...])


def _halo_specs(tm, width, seq):
    per = tm // SUBLANES
    last = seq // SUBLANES - 1
    return (pl.BlockSpec((SUBLANES, width), lambda i: (jnp.maximum(i * per - 1, 0), 0)),
            pl.BlockSpec((SUBLANES, width), lambda i: (jnp.minimum((i + 1) * per, last), 0)))


def _merge(h, u, ab, y_b, y_c, conv_w, w_gate, b_gate, w_branch, w_o, ln_g, ln_b, alpha, seq):
    tm = ROW_TILE
    nt = seq // tm
    row_spec = lambda w: pl.BlockSpec((tm, w), lambda i: (i, 0))
    halo_prev, halo_next = _halo_specs(tm, CONV_WIDTH, seq)
    return pl.pallas_call(
        functools.partial(_merge_kernel, n_tiles=nt, alpha=alpha),
        grid=(nt,),
        in_specs=[row_spec(D_MODEL), row_spec(CONV_WIDTH), halo_prev, halo_next,
                  row_spec(CONV_WIDTH), row_spec(DIFF_V), row_spec(SWA_Q),
                  _const_spec((3, CONV_WIDTH)),
                  _const_spec((D_MODEL, N_BRANCH * D_MODEL)), _const_spec((1, N_BRANCH * D_MODEL)),
                  _const_spec((N_BRANCH, BRANCH_WIDTH, D_MODEL)), _const_spec((D_MODEL, D_MODEL)),
                  _const_spec((1, D_MODEL)), _const_spec((1, D_MODEL))],
        out_specs=row_spec(D_MODEL),
        out_shape=jax.ShapeDtypeStruct((seq, D_MODEL), F32),
        compiler_params=_cparams(("arbitrary",)),
        name="merge",
    )(h, u, u, u, ab, y_b, y_c, conv_w, w_gate, b_gate.reshape(1, -1), w_branch, w_o,
      ln_g.reshape(1, D_MODEL), ln_b.reshape(1, D_MODEL))


def _gelu_exact(x):
    return 0.5 * x * (1.0 + lax.erf(x * (2.0 ** -0.5)))


def _ffn_kernel(h_ref, hp_ref, hn_ref, wu_ref, cw_ref, wd_ref, g_ref, b_ref, o_ref, hx_ref,
                *, n_tiles, alpha):
    i = pl.program_id(0)
    tm = h_ref.shape[0]
    pad = SUBLANES
    h = h_ref[...]
    hx_ref[0:pad, :] = jnp.where(i > 0, hp_ref[...], 0.0).astype(BF16)
    hx_ref[pad:pad + tm, :] = h.astype(BF16)
    hx_ref[pad + tm:, :] = jnp.where(i < n_tiles - 1, hn_ref[...], 0.0).astype(BF16)
    hx = hx_ref[...]
    n = tm + 2 * pad

    def conv(x, col):
        w = cw_ref[:, col:col + FFN_CHUNK]
        up = pltpu.roll(x, 1, axis=0)
        dn = pltpu.roll(x, n - 1, axis=0)
        y = w[0:1] * up + w[1:2] * x + w[2:3] * dn
        return y[pad:pad + tm]

    acc = None
    for c in range(D_FF // FFN_CHUNK):
        gc, uc = c * FFN_CHUNK, D_FF + c * FFN_CHUNK
        gate = conv(jnp.dot(hx, wu_ref[:, gc:gc + FFN_CHUNK], preferred_element_type=F32), gc)
        up = conv(jnp.dot(hx, wu_ref[:, uc:uc + FFN_CHUNK], preferred_element_type=F32), uc)
        act = (_gelu_exact(gate) * up).astype(BF16)
        part = jnp.dot(act, wd_ref[gc:gc + FFN_CHUNK, :], preferred_element_type=F32)
        acc = part if acc is None else acc + part
    o_ref[...] = _layer_norm(alpha * h + acc, g_ref[...], b_ref[...])


def _ffn(h, w_up, conv_w, w_down, ln_g, ln_b, alpha, seq):
    tm = ROW_TILE
    nt = seq // tm
    row_spec = pl.BlockSpec((tm, D_MODEL), lambda i: (i, 0))
    halo_prev, halo_next = _halo_specs(tm, D_MODEL, seq)
    return pl.pallas_call(
        functools.partial(_ffn_kernel, n_tiles=nt, alpha=alpha),
        grid=(nt,),
        in_specs=[row_spec, halo_prev, halo_next,
                  _const_spec((D_MODEL, 2 * D_FF)), _const_spec((3, 2 * D_FF)),
                  _const_spec((D_FF, D_MODEL)),
                  _const_spec((1, D_MODEL)), _const_spec((1, D_MODEL))],
        out_specs=row_spec,
        out_shape=jax.ShapeDtypeStruct((seq, D_MODEL), F32),
        scratch_shapes=[pltpu.VMEM((tm + 2 * SUBLANES, D_MODEL), BF16)],
        compiler_params=_cparams(("arbitrary",)),
        name="conv_ffn",
    )(h, h, h, w_up, conv_w, w_down, ln_g.reshape(1, D_MODEL), ln_b.reshape(1, D_MODEL))


def _split_in_proj(w):
    d = w.shape[0]
    cw = CONV_WIDTH
    o = 3 * cw
    dq = w[:, o:o + DIFF_QK]; o += DIFF_QK
    dk = w[:, o:o + DIFF_QK]; o += DIFF_QK
    dv = w[:, o:o + DIFF_V]; o += DIFF_V
    sq = w[:, o:o + SWA_Q]; o += SWA_Q
    sk = w[:, o:o + SWA_KV]; o += SWA_KV
    sv = w[:, o:o + SWA_KV]

    def pair_perm(x):
        return x.reshape(d, DIFF_HEADS, 2, 2, HALF).transpose(0, 1, 3, 2, 4).reshape(d, DIFF_QK)

    group = SWA_Q_HEADS // SWA_KV_HEADS
    sq_p = sq.reshape(d, SWA_KV_HEADS, group, 2, HALF).transpose(0, 2, 3, 1, 4).reshape(d, SWA_Q)
    sk_p = sk.reshape(d, SWA_KV_HEADS, 2, HALF).transpose(0, 2, 1, 3).reshape(d, SWA_KV)
    w_nat = jnp.concatenate([w[:, :3 * cw], pair_perm(dk), sk_p], axis=1)
    w_tr = jnp.concatenate([pair_perm(dq), dv, sq_p, sv], axis=1).T
    return w_nat, w_tr


def kernel(x, positions, ln_in_g, ln_in_b, w_in, conv_w, diff_lambda, diff_subln_g, swa_sink,
           w_branch_gate, b_branch_gate, w_branch, w_o, ln_mix_g, ln_mix_b,
           w_ffn_up, ffn_conv_w, w_ffn_down, ln_ffn_g, ln_ffn_b):
    batch, seq, _ = x.shape
    assert batch == 1 and seq % ROW_TILE == 0 and seq % Q_TILE == 0
    depth = w_in.shape[0]
    alpha = (2 * depth) ** 0.25
    tables = _rope_tables(positions, seq)
    h = x.reshape(seq, D_MODEL)
    for l in range(depth):
        lambda_init = 0.8 - 0.6 * math.exp(-0.3 * l)
        w_nat, w_tr = _split_in_proj(w_in[l].astype(BF16))
        if l == 0:
            h, *proj = _project(h, (ln_in_g, ln_in_b), w_nat, w_tr, tables, seq)
        else:
            proj = _project(h, None, w_nat, w_tr, tables, seq)
        u, ab, dk, sk, dqt, dvt, sqt, svt = proj
        y_b = _diff_attention(diff_lambda[l], diff_subln_g[l], dqt, dk, dvt, lambda_init, seq)
        y_c = _window_attention(swa_sink[l], sqt, sk, svt, seq)
        h = _merge(h, u, ab, y_b, y_c, conv_w[l], w_branch_gate[l].astype(BF16),
                   b_branch_gate[l], w_branch[l].astype(BF16), w_o[l].astype(BF16),
                   ln_mix_g[l], ln_mix_b[l], alpha, seq)
        h = _ffn(h, w_ffn_up[l].astype(BF16), ffn_conv_w[l], w_ffn_down[l].astype(BF16),
                 ln_ffn_g[l], ln_ffn_b[l], alpha, seq)
    return h.reshape(batch, seq, D_MODEL)
```

```python
import functools
import math

import jax
import jax.numpy as jnp
from jax import lax
from jax.experimental import pallas as pl
from jax.experimental.pallas import tpu as pltpu

F32 = jnp.float32
BF16 = jnp.bfloat16

D_MODEL = 1024
HEAD_DIM = 64
HALF = HEAD_DIM // 2
ROPE_THETA = 10000.0
CONV_WIDTH = 512
DIFF_HEADS = 4
DIFF_V_DIM = 2 * HEAD_DIM
SWA_Q_HEADS = 8
SWA_KV_HEADS = 2
SWA_WINDOW = 128
N_BRANCH = 3
BRANCH_WIDTH = 512
D_FF = 2816
LN_EPS = 1e-5
LANES = 128
SUBLANES = 8

DIFF_QK = DIFF_HEADS * 2 * HEAD_DIM
DIFF_V = DIFF_HEADS * DIFF_V_DIM
SWA_Q = SWA_Q_HEADS * HEAD_DIM
SWA_KV = SWA_KV_HEADS * HEAD_DIM
N_NAT = 3 * CONV_WIDTH + DIFF_QK + SWA_KV
N_TR = DIFF_QK + DIFF_V + SWA_Q + SWA_KV

ROW_TILE = 512
Q_TILE = 256
FFN_CHUNK = 256
VMEM_LIMIT = 56 * 1024 * 1024
NEG_BIG = -1e30

LOG2E = math.log2(math.e)
Q_SCALE = HEAD_DIM ** -0.5 * LOG2E


def _cparams(sem):
    return pltpu.CompilerParams(dimension_semantics=sem, vmem_limit_bytes=VMEM_LIMIT)


def _const_spec(shape):
    nd = len(shape)
    return pl.BlockSpec(shape, lambda *_: (0,) * nd, pipeline_mode=pl.Buffered(1))


def _layer_norm(x, g, b):
    mu = jnp.mean(x, axis=-1, keepdims=True)
    xc = x - mu
    var = jnp.mean(xc * xc, axis=-1, keepdims=True)
    return xc * lax.rsqrt(var + LN_EPS) * g + b


def _rope_table_kernel(pos_col_ref, pos_row_ref, inv_row_ref, inv_col_ref,
                       cos_n_ref, sin_n_ref, cos_t_ref, sin_t_ref):
    ang_n = pos_col_ref[...].astype(F32) * inv_row_ref[...]
    lane = lax.broadcasted_iota(jnp.int32, ang_n.shape, 1)
    cos_n_ref[...] = jnp.cos(ang_n)
    sin_n_ref[...] = jnp.where(lane < 2 * HALF, -jnp.sin(ang_n), jnp.sin(ang_n))
    ang_t = inv_col_ref[...] * pos_row_ref[...].astype(F32)
    row = lax.broadcasted_iota(jnp.int32, ang_t.shape, 0)
    cos_t_ref[...] = jnp.cos(ang_t)
    sin_t_ref[...] = jnp.where(row < 2 * HALF, -jnp.sin(ang_t), jnp.sin(ang_t))


def _rope_tables(positions, seq):
    ts = ROW_TILE
    inv_freq = 1.0 / (ROPE_THETA ** (jnp.arange(0, HEAD_DIM, 2, dtype=F32) / HEAD_DIM))
    inv_row = jnp.tile(inv_freq, LANES // HALF).reshape(1, LANES)
    inv_col = inv_row.reshape(LANES, 1)
    pos_col = positions.reshape(seq, 1)
    pos_row = positions.reshape(1, seq)
    return pl.pallas_call(
        _rope_table_kernel,
        grid=(seq // ts,),
        in_specs=[pl.BlockSpec((ts, 1), lambda i: (i, 0)),
                  pl.BlockSpec((1, ts), lambda i: (0, i)),
                  _const_spec((1, LANES)), _const_spec((LANES, 1))],
        out_specs=[pl.BlockSpec((ts, LANES), lambda i: (i, 0)),
                   pl.BlockSpec((ts, LANES), lambda i: (i, 0)),
                   pl.BlockSpec((LANES, ts), lambda i: (0, i)),
                   pl.BlockSpec((LANES, ts), lambda i: (0, i))],
        out_shape=[jax.ShapeDtypeStruct((seq, LANES), F32),
                   jax.ShapeDtypeStruct((seq, LANES), F32),
                   jax.ShapeDtypeStruct((LANES, seq), F32),
                   jax.ShapeDtypeStruct((LANES, seq), F32)],
        compiler_params=_cparams(("arbitrary",)),
        name="rope_tables",
    )(pos_col, pos_row, inv_row, inv_col)


def _rope_rows(x, cos_n, sin_n):
    return x * cos_n + pltpu.roll(x, 2 * HALF, axis=1) * sin_n


def _rope_cols(x, cos_t, sin_t):
    swapped = jnp.concatenate([x[2 * HALF:], x[:2 * HALF]], axis=0)
    return x * cos_t + swapped * sin_t


def _proj_kernel(*refs, apply_ln):
    if apply_ln:
        x_ref, g_ref, b_ref = refs[:3]
        refs = refs[3:]
        h = _layer_norm(x_ref[...], g_ref[...], b_ref[...])
    else:
        x_ref = refs[0]
        refs = refs[1:]
        h = x_ref[...]
    (wn_ref, wt_ref, cos_n_ref, sin_n_ref, cos_t_ref, sin_t_ref) = refs[:6]
    outs = refs[6:]
    if apply_ln:
        h_out_ref = outs[0]
        outs = outs[1:]
        h_out_ref[...] = h
    u_ref, ab_ref, dk_ref, sk_ref, dqt_ref, dvt_ref, sqt_ref, svt_ref = outs

    hb = h.astype(BF16)
    cos_n, sin_n = cos_n_ref[...], sin_n_ref[...]
    cos_t, sin_t = cos_t_ref[...], sin_t_ref[...]

    def nat(lo, hi):
        return jnp.dot(hb, wn_ref[:, lo:hi], preferred_element_type=F32)

    tr_all = lax.dot_general(wt_ref[...], hb, (((1,), (1,)), ((), ())),
                             preferred_element_type=F32)

    def tr(lo, hi):
        return tr_all[lo:hi]

    cw = CONV_WIDTH
    a_x = nat(0, cw)
    a_c = nat(2 * cw, 3 * cw)
    u_ref[...] = (a_c * a_x).astype(BF16)
    ab_ref[...] = nat(cw, 2 * cw).astype(BF16)

    base = 3 * cw
    for hh in range(DIFF_HEADS):
        xk = nat(base + hh * LANES, base + (hh + 1) * LANES)
        dk_ref[hh] = _rope_rows(xk, cos_n, sin_n).astype(BF16)
    base += DIFF_QK
    sk_ref[...] = _rope_rows(nat(base, base + SWA_KV), cos_n, sin_n).astype(BF16)

    for hh in range(DIFF_HEADS):
        xq = tr(hh * LANES, (hh + 1) * LANES)
        dqt_ref[hh] = (_rope_cols(xq, cos_t, sin_t) * Q_SCALE).astype(BF16)
    base = DIFF_QK
    for hh in range(DIFF_HEADS):
        dvt_ref[hh, 0] = tr(base + hh * LANES, base + (hh + 1) * LANES).astype(BF16)
    base += DIFF_V
    for t in range(SWA_Q // LANES):
        xq = tr(base + t * LANES, base + (t + 1) * LANES)
        sqt_ref[t] = (_rope_cols(xq, cos_t, sin_t) * Q_SCALE).astype(BF16)
    base += SWA_Q
    svt_ref[...] = tr(base, base + SWA_KV).astype(BF16)


def _project(h_or_x, ln_gb, w_nat, w_tr, tables, seq):
    tm = ROW_TILE
    nt = seq // tm
    apply_ln = ln_gb is not None
    cos_n, sin_n, cos_t, sin_t = tables
    row_spec = lambda w: pl.BlockSpec((tm, w), lambda i: (i, 0))
    in_specs = [row_spec(D_MODEL)]
    args = [h_or_x]
    if apply_ln:
        in_specs += [_const_spec((1, D_MODEL)), _const_spec((1, D_MODEL))]
        args += [ln_gb[0].reshape(1, D_MODEL), ln_gb[1].reshape(1, D_MODEL)]
    in_specs += [_const_spec((D_MODEL, N_NAT)), _const_spec((N_TR, D_MODEL)),
                 row_spec(LANES), row_spec(LANES),
                 pl.BlockSpec((LANES, tm), lambda i: (0, i)),
                 pl.BlockSpec((LANES, tm), lambda i: (0, i))]
    args += [w_nat, w_tr, cos_n, sin_n, cos_t, sin_t]
    out_specs, out_shape = [], []
    if apply_ln:
        out_specs.append(row_spec(D_MODEL))
        out_shape.append(jax.ShapeDtypeStruct((seq, D_MODEL), F32))
    out_specs += [
        row_spec(CONV_WIDTH), row_spec(CONV_WIDTH),
        pl.BlockSpec((DIFF_HEADS, tm, LANES), lambda i: (0, i, 0)),
        row_spec(LANES),
        pl.BlockSpec((DIFF_HEADS, LANES, tm), lambda i: (0, 0, i)),
        pl.BlockSpec((DIFF_HEADS, 1, LANES, tm), lambda i: (0, i, 0, 0)),
        pl.BlockSpec((SWA_Q // LANES, LANES, tm), lambda i: (0, 0, i)),
        pl.BlockSpec((LANES, tm), lambda i: (0, i)),
    ]
    out_shape += [
        jax.ShapeDtypeStruct((seq, CONV_WIDTH), BF16),
        jax.ShapeDtypeStruct((seq, CONV_WIDTH), BF16),
        jax.ShapeDtypeStruct((DIFF_HEADS, seq, LANES), BF16),
        jax.ShapeDtypeStruct((seq, LANES), BF16),
        jax.ShapeDtypeStruct((DIFF_HEADS, LANES, seq), BF16),
        jax.ShapeDtypeStruct((DIFF_HEADS, nt, LANES, tm), BF16),
        jax.ShapeDtypeStruct((SWA_Q // LANES, LANES, seq), BF16),
        jax.ShapeDtypeStruct((LANES, seq), BF16),
    ]
    return pl.pallas_call(
        functools.partial(_proj_kernel, apply_ln=apply_ln),
        grid=(nt,), in_specs=in_specs, out_specs=out_specs, out_shape=out_shape,
        compiler_params=_cparams(("arbitrary",)),
        name="proj_ln" if apply_ln else "proj",
    )(*args)


def _pair_masks(qt):
    row = lax.broadcasted_iota(jnp.int32, qt.shape, 0)
    is_a = (row & HALF) == 0
    zero = jnp.zeros_like(qt)
    return jnp.where(is_a, qt, zero), jnp.where(is_a, zero, qt)


def _diff_attn_kernel(lam_ref, g_ref, qt_ref, k_ref, vt_ref, o_ref, acc_ref, s0_ref, s1_ref,
                      *, lambda_init, tk, n_kv):
    qs = _pair_masks(qt_ref[...])
    tq = qs[0].shape[1]
    acc_ref[...] = jnp.zeros_like(acc_ref)

    def scores(j, s_ref):
        off = pl.multiple_of(j * tk, tk)
        kb = k_ref[pl.ds(off, tk), :]
        cms = []
        for x in range(2):
            s = jnp.dot(kb, qs[x], preferred_element_type=F32)
            s_ref[x] = s
            cms.append(jnp.max(s, axis=0, keepdims=True))
        return tuple(cms)

    def consume(j, s_ref, cm, m, l):
        vb = vt_ref[j]
        m_out, l_out = [], []
        for x in range(2):
            m_new = jnp.maximum(m[x], cm[x])
            alpha = jnp.exp2(m[x] - m_new)
            p = jnp.exp2(s_ref[x] - m_new)
            l_out.append(alpha * l[x] + jnp.sum(p, axis=0, keepdims=True))
            m_out.append(m_new)
            acc_ref[x] = alpha * acc_ref[x] + jnp.dot(vb, p.astype(BF16),
                                                      preferred_element_type=F32)
        return tuple(m_out), tuple(l_out)

    def body(t, carry):
        cm0, m, l = carry
        j = 2 * t
        cm1 = scores(j + 1, s1_ref)
        m, l = consume(j, s0_ref, cm0, m, l)
        cm0 = scores(jnp.minimum(j + 2, n_kv - 1), s0_ref)
        m, l = consume(j + 1, s1_ref, cm1, m, l)
        return cm0, m, l

    m0 = jnp.full((1, tq), NEG_BIG, F32)
    l0 = jnp.zeros((1, tq), F32)
    _, _, (la, lb) = lax.fori_loop(0, n_kv // 2, body,
                                   (scores(0, s0_ref), (m0, m0), (l0, l0)))

    lp = lam_ref[...]
    s01 = jnp.sum(lp[0:1] * lp[1:2], axis=1, keepdims=True)
    s23 = jnp.sum(lp[2:3] * lp[3:4], axis=1, keepdims=True)
    lam = jnp.exp(s01) - jnp.exp(s23) + lambda_init
    o = acc_ref[0] * (1.0 / la) - lam * (acc_ref[1] * (1.0 / lb))
    ms = jnp.mean(o * o, axis=0, keepdims=True)
    y = o * lax.rsqrt(ms + LN_EPS) * g_ref[...] * (1.0 - lambda_init)
    o_ref[...] = y.T.astype(BF16)


def _diff_attention(lam_p, subln_g, dqt, dk, dvt, lambda_init, seq):
    tq = Q_TILE
    n_kv, tk = dvt.shape[1], dvt.shape[3]
    assert n_kv % 2 == 0
    return pl.pallas_call(
        functools.partial(_diff_attn_kernel, lambda_init=lambda_init, tk=tk, n_kv=n_kv),
        grid=(DIFF_HEADS, seq // tq),
        in_specs=[_const_spec(lam_p.shape), _const_spec((DIFF_V_DIM, 1)),
                  pl.BlockSpec((None, LANES, tq), lambda h, i: (h, 0, i)),
                  pl.BlockSpec((None, seq, LANES), lambda h, i: (h, 0, 0)),
                  pl.BlockSpec((None, n_kv, LANES, tk), lambda h, i: (h, 0, 0, 0))],
        out_specs=pl.BlockSpec((tq, DIFF_V_DIM), lambda h, i: (i, h)),
        out_shape=jax.ShapeDtypeStruct((seq, DIFF_V), BF16),
        scratch_shapes=[pltpu.VMEM((2, DIFF_V_DIM, tq), F32),
                        pltpu.VMEM((2, tk, tq), F32), pltpu.VMEM((2, tk, tq), F32)],
        compiler_params=_cparams(("arbitrary", "arbitrary")),
        name="diff_attn",
    )(lam_p, subln_g.reshape(DIFF_V_DIM, 1), dqt, dk, dvt)


def _window_attn_kernel(sink_ref, qt_ref, kp_ref, km_ref, kn_ref, vp_ref, vm_ref, vn_ref, o_ref,
                        *, n_blocks):
    i = pl.program_id(0)
    tq = qt_ref.shape[2]
    w = SWA_WINDOW
    kcat = jnp.concatenate([kp_ref[...], km_ref[...], kn_ref[...]], axis=0)
    vcat = jnp.concatenate([vp_ref[...], vm_ref[...], vn_ref[...]], axis=1)
    nk = tq + 2 * w
    r = lax.broadcasted_iota(jnp.int32, (nk, tq), 0)
    c = lax.broadcasted_iota(jnp.int32, (nk, tq), 1)
    r_lo = jnp.where(i > 0, 0, w)
    r_hi = jnp.where(i < n_blocks - 1, nk, tq + w)
    valid = (r >= jnp.maximum(c, r_lo)) & (r <= c + 2 * w) & (r < r_hi)
    sink = sink_ref[...] * LOG2E
    outs = [None] * SWA_Q_HEADS
    group = SWA_Q_HEADS // SWA_KV_HEADS
    for t in range(group):
        qa, qb = _pair_masks(qt_ref[t])
        for c_kv, qx in ((0, qa), (1, qb)):
            head = c_kv * group + t
            s = jnp.dot(kcat, qx, preferred_element_type=F32)
            s = jnp.where(valid, s, NEG_BIG)
            sk = sink[:, head:head + 1]
            m = jnp.maximum(jnp.max(s, axis=0, keepdims=True), sk)
            p = jnp.exp2(s - m)
            denom = jnp.sum(p, axis=0, keepdims=True) + jnp.exp2(sk - m)
            vt = vcat[c_kv * HEAD_DIM:(c_kv + 1) * HEAD_DIM, :]
            ot = jnp.dot(vt, p.astype(BF16), preferred_element_type=F32)
            outs[head] = ot * (1.0 / denom)
    o_ref[...] = jnp.concatenate(outs, axis=0).T.astype(BF16)


def _window_attention(sink, sqt, sk, svt, seq):
    tq = Q_TILE
    w = SWA_WINDOW
    nb = seq // tq
    per = tq // w
    last = seq // w - 1
    prev_idx = lambda i: jnp.maximum(i * per - 1, 0)
    next_idx = lambda i: jnp.minimum((i + 1) * per, last)
    return pl.pallas_call(
        functools.partial(_window_attn_kernel, n_blocks=nb),
        grid=(nb,),
        in_specs=[_const_spec((1, SWA_Q_HEADS)),
                  pl.BlockSpec((SWA_Q // LANES, LANES, tq), lambda i: (0, 0, i)),
                  pl.BlockSpec((w, LANES), lambda i: (prev_idx(i), 0)),
                  pl.BlockSpec((tq, LANES), lambda i: (i, 0)),
                  pl.BlockSpec((w, LANES), lambda i: (next_idx(i), 0)),
                  pl.BlockSpec((LANES, w), lambda i: (0, prev_idx(i))),
                  pl.BlockSpec((LANES, tq), lambda i: (0, i)),
                  pl.BlockSpec((LANES, w), lambda i: (0, next_idx(i)))],
        out_specs=pl.BlockSpec((tq, SWA_Q), lambda i: (i, 0)),
        out_shape=jax.ShapeDtypeStruct((seq, SWA_Q), BF16),
        compiler_params=_cparams(("arbitrary",)),
        name="window_attn",
    )(sink.reshape(1, SWA_Q_HEADS), sqt, sk, sk, sk, svt, svt, svt)


def _shift_rows(x, prev_row, next_row):
    n = x.shape[0]
    row = lax.broadcasted_iota(jnp.int32, x.shape, 0)
    up = jnp.where(row == 0, prev_row, pltpu.roll(x, 1, axis=0))
    down = jnp.where(row == n - 1, next_row, pltpu.roll(x, n - 1, axis=0))
    return up, down


def _merge_kernel(h_ref, u_ref, up_ref, un_ref, ab_ref, yb_ref, yc_ref, cw_ref,
                  wg_ref, bg_ref, wb_ref, wo_ref, g_ref, b_ref, o_ref, *, n_tiles, alpha):
    i = pl.program_id(0)
    h = h_ref[...]
    hb = h.astype(BF16)
    u = u_ref[...].astype(F32)
    prev_row = jnp.where(i > 0, up_ref[SUBLANES - 1:SUBLANES, :].astype(F32), 0.0)
    next_row = jnp.where(i < n_tiles - 1, un_ref[0:1, :].astype(F32), 0.0)
    u_up, u_dn = _shift_rows(u, prev_row, next_row)
    cw = cw_ref[...]
    y_a = ab_ref[...].astype(F32) * (cw[0:1] * u_up + cw[1:2] * u + cw[2:3] * u_dn)
    ys = (y_a.astype(BF16), yb_ref[...], yc_ref[...])
    mix = None
    for n in range(N_BRANCH):
        lo, hi = n * D_MODEL, (n + 1) * D_MODEL
        gate = jax.nn.sigmoid(jnp.dot(hb, wg_ref[:, lo:hi], preferred_element_type=F32)
                              + bg_ref[:, lo:hi])
        term = gate * jnp.dot(ys[n], wb_ref[n], preferred_element_type=F32)
        mix = term if mix is None else mix + term
    out = jnp.dot(mix.astype(BF16), wo_ref[...], preferred_element_type=F32)
    o_ref[...] = _layer_norm(alpha * h + out, g_ref[...], b_ref[...])


def _halo_specs(tm, width, seq):
    per = tm // SUBLANES
    last = seq // SUBLANES - 1
    return (pl.BlockSpec((SUBLANES, width), lambda i: (jnp.maximum(i * per - 1, 0), 0)),
            pl.BlockSpec((SUBLANES, width), lambda i: (jnp.minimum((i + 1) * per, last), 0)))


def _merge(h, u, ab, y_b, y_c, conv_w, w_gate, b_gate, w_branch, w_o, ln_g, ln_b, alpha, seq):
    tm = ROW_TILE
    nt = seq // tm
    row_spec = lambda w: pl.BlockSpec((tm, w), lambda i: (i, 0))
    halo_prev, halo_next = _halo_specs(tm, CONV_WIDTH, seq)
    return pl.pallas_call(
        functools.partial(_merge_kernel, n_tiles=nt, alpha=alpha),
        grid=(nt,),
        in_specs=[row_spec(D_MODEL), row_spec(CONV_WIDTH), halo_prev, halo_next,
                  row_spec(CONV_WIDTH), row_spec(DIFF_V), row_spec(SWA_Q),
                  _const_spec((3, CONV_WIDTH)),
                  _const_spec((D_MODEL, N_BRANCH * D_MODEL)), _const_spec((1, N_BRANCH * D_MODEL)),
                  _const_spec((N_BRANCH, BRANCH_WIDTH, D_MODEL)), _const_spec((D_MODEL, D_MODEL)),
                  _const_spec((1, D_MODEL)), _const_spec((1, D_MODEL))],
        out_specs=row_spec(D_MODEL),
        out_shape=jax.ShapeDtypeStruct((seq, D_MODEL), F32),
        compiler_params=_cparams(("arbitrary",)),
        name="merge",
    )(h, u, u, u, ab, y_b, y_c, conv_w, w_gate, b_gate.reshape(1, -1), w_branch, w_o,
      ln_g.reshape(1, D_MODEL), ln_b.reshape(1, D_MODEL))


def _gelu_exact(x):
    return 0.5 * x * (1.0 + lax.erf(x * (2.0 ** -0.5)))


def _ffn_kernel(h_ref, hp_ref, hn_ref, wu_ref, cw_ref, wd_ref, g_ref, b_ref, o_ref, hx_ref,
                *, n_tiles, alpha):
    i = pl.program_id(0)
    tm = h_ref.shape[0]
    pad = SUBLANES
    h = h_ref[...]
    hx_ref[0:pad, :] = jnp.where(i > 0, hp_ref[...], 0.0).astype(BF16)
    hx_ref[pad:pad + tm, :] = h.astype(BF16)
    hx_ref[pad + tm:, :] = jnp.where(i < n_tiles - 1, hn_ref[...], 0.0).astype(BF16)
    hx = hx_ref[...]
    n = tm + 2 * pad

    def conv(x, col):
        w = cw_ref[:, col:col + FFN_CHUNK]
        up = pltpu.roll(x, 1, axis=0)
        dn = pltpu.roll(x, n - 1, axis=0)
        y = w[0:1] * up + w[1:2] * x + w[2:3] * dn
        return y[pad:pad + tm]

    acc = None
    for c in range(D_FF // FFN_CHUNK):
        gc, uc = c * FFN_CHUNK, D_FF + c * FFN_CHUNK
        gate = conv(jnp.dot(hx, wu_ref[:, gc:gc + FFN_CHUNK], preferred_element_type=F32), gc)
        up = conv(jnp.dot(hx, wu_ref[:, uc:uc + FFN_CHUNK], preferred_element_type=F32), uc)
        act = (_gelu_exact(gate) * up).astype(BF16)
        part = jnp.dot(act, wd_ref[gc:gc + FFN_CHUNK, :], preferred_element_type=F32)
        acc = part if acc is None else acc + part
    o_ref[...] = _layer_norm(alpha * h + acc, g_ref[...], b_ref[...])


def _ffn(h, w_up, conv_w, w_down, ln_g, ln_b, alpha, seq):
    tm = ROW_TILE
    nt = seq // tm
    row_spec = pl.BlockSpec((tm, D_MODEL), lambda i: (i, 0))
    halo_prev, halo_next = _halo_specs(tm, D_MODEL, seq)
    return pl.pallas_call(
        functools.partial(_ffn_kernel, n_tiles=nt, alpha=alpha),
        grid=(nt,),
        in_specs=[row_spec, halo_prev, halo_next,
                  _const_spec((D_MODEL, 2 * D_FF)), _const_spec((3, 2 * D_FF)),
                  _const_spec((D_FF, D_MODEL)),
                  _const_spec((1, D_MODEL)), _const_spec((1, D_MODEL))],
        out_specs=row_spec,
        out_shape=jax.ShapeDtypeStruct((seq, D_MODEL), F32),
        scratch_shapes=[pltpu.VMEM((tm + 2 * SUBLANES, D_MODEL), BF16)],
        compiler_params=_cparams(("arbitrary",)),
        name="conv_ffn",
    )(h, h, h, w_up, conv_w, w_down, ln_g.reshape(1, D_MODEL), ln_b.reshape(1, D_MODEL))


def _split_in_proj(w):
    d = w.shape[0]
    cw = CONV_WIDTH
    o = 3 * cw
    dq = w[:, o:o + DIFF_QK]; o += DIFF_QK
    dk = w[:, o:o + DIFF_QK]; o += DIFF_QK
    dv = w[:, o:o + DIFF_V]; o += DIFF_V
    sq = w[:, o:o + SWA_Q]; o += SWA_Q
    sk = w[:, o:o + SWA_KV]; o += SWA_KV
    sv = w[:, o:o + SWA_KV]

    def pair_perm(x):
        return x.reshape(d, DIFF_HEADS, 2, 2, HALF).transpose(0, 1, 3, 2, 4).reshape(d, DIFF_QK)

    group = SWA_Q_HEADS // SWA_KV_HEADS
    sq_p = sq.reshape(d, SWA_KV_HEADS, group, 2, HALF).transpose(0, 2, 3, 1, 4).reshape(d, SWA_Q)
    sk_p = sk.reshape(d, SWA_KV_HEADS, 2, HALF).transpose(0, 2, 1, 3).reshape(d, SWA_KV)
    w_nat = jnp.concatenate([w[:, :3 * cw], pair_perm(dk), sk_p], axis=1)
    w_tr = jnp.concatenate([pair_perm(dq), dv, sq_p, sv], axis=1).T
    return w_nat, w_tr


def kernel(x, positions, ln_in_g, ln_in_b, w_in, conv_w, diff_lambda, diff_subln_g, swa_sink,
           w_branch_gate, b_branch_gate, w_branch, w_o, ln_mix_g, ln_mix_b,
           w_ffn_up, ffn_conv_w, w_ffn_down, ln_ffn_g, ln_ffn_b):
    batch, seq, _ = x.shape
    assert batch == 1 and seq % ROW_TILE == 0 and seq % Q_TILE == 0
    depth = w_in.shape[0]
    alpha = (2 * depth) ** 0.25
    tables = _rope_tables(positions, seq)
    h = x.reshape(seq, D_MODEL)
    for l in range(depth):
        lambda_init = 0.8 - 0.6 * math.exp(-0.3 * l)
        w_nat, w_tr = _split_in_proj(w_in[l].astype(BF16))
        if l == 0:
            h, *proj = _project(h, (ln_in_g, ln_in_b), w_nat, w_tr, tables, seq)
        else:
            proj = _project(h, None, w_nat, w_tr, tables, seq)
        u, ab, dk, sk, dqt, dvt, sqt, svt = proj
        y_b = _diff_attention(diff_lambda[l], diff_subln_g[l], dqt, dk, dvt, lambda_init, seq)
        y_c = _window_attention(swa_sink[l], sqt, sk, svt, seq)
        h = _merge(h, u, ab, y_b, y_c, conv_w[l], w_branch_gate[l].astype(BF16),
                   b_branch_gate[l], w_branch[l].astype(BF16), w_o[l].astype(BF16),
                   ln_mix_g[l], ln_mix_b[l], alpha, seq)
        h = _ffn(h, w_ffn_up[l].astype(BF16), ffn_conv_w[l], w_ffn_down[l].astype(BF16),
                 ln_ffn_g[l], ln_ffn_b[l], alpha, seq)
    return h.reshape(batch, seq, D_MODEL)
```

```python
import functools
import math

import jax
import jax.numpy as jnp
from jax import lax
from jax.experimental import pallas as pl
from jax.experimental.pallas import tpu as pltpu

F32 = jnp.float32
BF16 = jnp.bfloat16

D_MODEL = 1024
HEAD_DIM = 64
HALF = HEAD_DIM // 2
ROPE_THETA = 10000.0
CONV_WIDTH = 512
DIFF_HEADS = 4
DIFF_V_DIM = 2 * HEAD_DIM
SWA_Q_HEADS = 8
SWA_KV_HEADS = 2
SWA_WINDOW = 128
N_BRANCH = 3
BRANCH_WIDTH = 512
D_FF = 2816
LN_EPS = 1e-5
LANES = 128
SUBLANES = 8

DIFF_QK = DIFF_HEADS * 2 * HEAD_DIM
DIFF_V = DIFF_HEADS * DIFF_V_DIM
SWA_Q = SWA_Q_HEADS * HEAD_DIM
SWA_KV = SWA_KV_HEADS * HEAD_DIM
N_NAT = 3 * CONV_WIDTH + DIFF_QK + SWA_KV
N_TR = DIFF_QK + DIFF_V + SWA_Q + SWA_KV

ROW_TILE = 512
Q_TILE = 256
KV_UNROLL = 8
SUM_ROWS = 16
VMEM_LIMIT = 56 * 1024 * 1024
NEG_BIG = -1e30

LOG2E = math.log2(math.e)
Q_SCALE = HEAD_DIM ** -0.5 * LOG2E


def _cparams(sem):
    return pltpu.CompilerParams(dimension_semantics=sem, vmem_limit_bytes=VMEM_LIMIT)


def _const_spec(shape):
    nd = len(shape)
    return pl.BlockSpec(shape, lambda *_: (0,) * nd, pipeline_mode=pl.Buffered(1))


def _layer_norm(x, g, b):
    mu = jnp.mean(x, axis=-1, keepdims=True)
    xc = x - mu
    var = jnp.mean(xc * xc, axis=-1, keepdims=True)
    return xc * lax.rsqrt(var + LN_EPS) * g + b


def _rope_table_kernel(pos_ref, inv_ref, cos_n_ref, sin_n_ref, cos_t_ref, sin_t_ref):
    ang = inv_ref[...] * pos_ref[...].astype(F32)
    c, s = jnp.cos(ang), jnp.sin(ang)
    cos_t = jnp.concatenate([c, c, c, c], axis=0)
    sin_t = jnp.concatenate([-s, -s, s, s], axis=0)
    cos_t_ref[...] = cos_t
    sin_t_ref[...] = sin_t
    cos_n_ref[...] = cos_t.T
    sin_n_ref[...] = sin_t.T


def _rope_tables(positions, seq):
    ts = ROW_TILE
    inv_freq = 1.0 / (ROPE_THETA ** (jnp.arange(0, HEAD_DIM, 2, dtype=F32) / HEAD_DIM))
    return pl.pallas_call(
        _rope_table_kernel,
        grid=(seq // ts,),
        in_specs=[pl.BlockSpec((1, ts), lambda i: (0, i)), _const_spec((HALF, 1))],
        out_specs=[pl.BlockSpec((ts, LANES), lambda i: (i, 0)),
                   pl.BlockSpec((ts, LANES), lambda i: (i, 0)),
                   pl.BlockSpec((LANES, ts), lambda i: (0, i)),
                   pl.BlockSpec((LANES, ts), lambda i: (0, i))],
        out_shape=[jax.ShapeDtypeStruct((seq, LANES), F32),
                   jax.ShapeDtypeStruct((seq, LANES), F32),
                   jax.ShapeDtypeStruct((LANES, seq), F32),
                   jax.ShapeDtypeStruct((LANES, seq), F32)],
        compiler_params=_cparams(("arbitrary",)),
        name="rope_tables",
    )(positions.reshape(1, seq), inv_freq.reshape(HALF, 1))


def _rope_rows(x, cos_n, sin_n):
    return x * cos_n + pltpu.roll(x, 2 * HALF, axis=1) * sin_n


def _rope_cols(x, cos_t, sin_t):
    swapped = jnp.concatenate([x[2 * HALF:], x[:2 * HALF]], axis=0)
    return x * cos_t + swapped * sin_t


def _proj_kernel(*refs, apply_ln):
    if apply_ln:
        x_ref, g_ref, b_ref = refs[:3]
        refs = refs[3:]
        h = _layer_norm(x_ref[...], g_ref[...], b_ref[...])
    else:
        x_ref = refs[0]
        refs = refs[1:]
        h = x_ref[...]
    (wn_ref, wt_ref, cos_n_ref, sin_n_ref, cos_t_ref, sin_t_ref) = refs[:6]
    outs = refs[6:]
    if apply_ln:
        h_out_ref = outs[0]
        outs = outs[1:]
        h_out_ref[...] = h
    u_ref, ab_ref, dk_ref, sk_ref, dqt_ref, dvt_ref, sqt_ref, svt_ref = outs

    hb = h.astype(BF16)
    cos_n, sin_n = cos_n_ref[...], sin_n_ref[...]
    cos_t, sin_t = cos_t_ref[...], sin_t_ref[...]

    def nat(lo, hi):
        return jnp.dot(hb, wn_ref[:, lo:hi], preferred_element_type=F32)

    tr_all = lax.dot_general(wt_ref[...], hb, (((1,), (1,)), ((), ())),
                             preferred_element_type=F32)

    def tr(lo, hi):
        return tr_all[lo:hi]

    cw = CONV_WIDTH
    a_x = nat(0, cw)
    a_c = nat(2 * cw, 3 * cw)
    u_ref[...] = (a_c * a_x).astype(BF16)
    ab_ref[...] = nat(cw, 2 * cw).astype(BF16)

    base = 3 * cw
    for hh in range(DIFF_HEADS):
        xk = nat(base + hh * LANES, base + (hh + 1) * LANES)
        dk_ref[hh] = _rope_rows(xk, cos_n, sin_n).astype(BF16)
    base += DIFF_QK
    sk_ref[...] = _rope_rows(nat(base, base + SWA_KV), cos_n, sin_n).astype(BF16)

    for hh in range(DIFF_HEADS):
        xq = tr(hh * LANES, (hh + 1) * LANES)
        dqt_ref[hh] = (_rope_cols(xq, cos_t, sin_t) * Q_SCALE).astype(BF16)
    base = DIFF_QK
    for hh in range(DIFF_HEADS):
        dvt_ref[hh, 0] = tr(base + hh * LANES, base + (hh + 1) * LANES).astype(BF16)
    base += DIFF_V
    for t in range(SWA_Q // LANES):
        xq = tr(base + t * LANES, base + (t + 1) * LANES)
        sqt_ref[t] = (_rope_cols(xq, cos_t, sin_t) * Q_SCALE).astype(BF16)
    base += SWA_Q
    svt_ref[...] = tr(base, base + SWA_KV).astype(BF16)


def _project(h_or_x, ln_gb, w_nat, w_tr, tables, seq):
    tm = ROW_TILE
    nt = seq // tm
    apply_ln = ln_gb is not None
    cos_n, sin_n, cos_t, sin_t = tables
    row_spec = lambda w: pl.BlockSpec((tm, w), lambda i: (i, 0))
    in_specs = [row_spec(D_MODEL)]
    args = [h_or_x]
    if apply_ln:
        in_specs += [_const_spec((1, D_MODEL)), _const_spec((1, D_MODEL))]
        args += [ln_gb[0].reshape(1, D_MODEL), ln_gb[1].reshape(1, D_MODEL)]
    in_specs += [_const_spec((D_MODEL, N_NAT)), _const_spec((N_TR, D_MODEL)),
                 row_spec(LANES), row_spec(LANES),
                 pl.BlockSpec((LANES, tm), lambda i: (0, i)),
                 pl.BlockSpec((LANES, tm), lambda i: (0, i))]
    args += [w_nat, w_tr, cos_n, sin_n, cos_t, sin_t]
    out_specs, out_shape = [], []
    if apply_ln:
        out_specs.append(row_spec(D_MODEL))
        out_shape.append(jax.ShapeDtypeStruct((seq, D_MODEL), F32))
    out_specs += [
        row_spec(CONV_WIDTH), row_spec(CONV_WIDTH),
        pl.BlockSpec((DIFF_HEADS, tm, LANES), lambda i: (0, i, 0)),
        row_spec(LANES),
        pl.BlockSpec((DIFF_HEADS, LANES, tm), lambda i: (0, 0, i)),
        pl.BlockSpec((DIFF_HEADS, 1, LANES, tm), lambda i: (0, i, 0, 0)),
        pl.BlockSpec((SWA_Q // LANES, LANES, tm), lambda i: (0, 0, i)),
        pl.BlockSpec((LANES, tm), lambda i: (0, i)),
    ]
    out_shape += [
        jax.ShapeDtypeStruct((seq, CONV_WIDTH), BF16),
        jax.ShapeDtypeStruct((seq, CONV_WIDTH), BF16),
        jax.ShapeDtypeStruct((DIFF_HEADS, seq, LANES), BF16),
        jax.ShapeDtypeStruct((seq, LANES), BF16),
        jax.ShapeDtypeStruct((DIFF_HEADS, LANES, seq), BF16),
        jax.ShapeDtypeStruct((DIFF_HEADS, nt, LANES, tm), BF16),
        jax.ShapeDtypeStruct((SWA_Q // LANES, LANES, seq), BF16),
        jax.ShapeDtypeStruct((LANES, seq), BF16),
    ]
    return pl.pallas_call(
        functools.partial(_proj_kernel, apply_ln=apply_ln),
        grid=(nt,), in_specs=in_specs, out_specs=out_specs, out_shape=out_shape,
        compiler_params=_cparams(("arbitrary",)),
        name="proj_ln" if apply_ln else "proj",
    )(*args)


def _pair_masks(qt):
    row = lax.broadcasted_iota(jnp.int32, qt.shape, 0)
    is_a = (row & HALF) == 0
    zero = jnp.zeros_like(qt)
    return jnp.where(is_a, qt, zero), jnp.where(is_a, zero, qt)


def _diff_attn_kernel(lam_ref, g_ref, qt_ref, k_ref, vt_ref, o_ref, acc_ref, s0_ref, s1_ref, q_ref,
                      *, lambda_init, tk, n_kv, unroll):
    qa, qb = _pair_masks(qt_ref[...])
    q_ref[0] = qa
    q_ref[1] = qb
    tq = qa.shape[1]
    dv = DIFF_V_DIM
    acc_ref[...] = jnp.zeros_like(acc_ref)
    ones = jnp.ones((SUM_ROWS, tk), BF16)
    s_bufs = (s0_ref, s1_ref)

    def scores(j, s_ref):
        off = pl.multiple_of(j * tk, tk)
        kb = k_ref[pl.ds(off, tk), :]
        cms = []
        for x in range(2):
            s = jnp.dot(kb, q_ref[x], preferred_element_type=F32)
            s_ref[x] = s
            cms.append(jnp.max(s, axis=0, keepdims=True))
        return tuple(cms)

    def consume(j, s_ref, cm, m):
        vb = jnp.concatenate([vt_ref[j], ones], axis=0)
        m_out = []
        for x in range(2):
            m_new = jnp.maximum(m[x], cm[x])
            alpha = jnp.exp2(m[x] - m_new)
            p = jnp.exp2(s_ref[x] - m_new).astype(BF16)
            m_out.append(m_new)
            acc_ref[x] = alpha * acc_ref[x] + jnp.dot(vb, p, preferred_element_type=F32)
        return tuple(m_out)

    def body(t, carry):
        cm, m = carry
        j = unroll * t
        for u in range(unroll):
            cm_next = scores(jnp.minimum(j + u + 1, n_kv - 1), s_bufs[(u + 1) % 2])
            m = consume(j + u, s_bufs[u % 2], cm, m)
            cm = cm_next
        return cm, m

    m0 = jnp.full((1, tq), NEG_BIG, F32)
    lax.fori_loop(0, n_kv // unroll, body, (scores(0, s0_ref), (m0, m0)))

    lp = lam_ref[...]
    s01 = jnp.sum(lp[0:1] * lp[1:2], axis=1, keepdims=True)
    s23 = jnp.sum(lp[2:3] * lp[3:4], axis=1, keepdims=True)
    lam = jnp.exp(s01) - jnp.exp(s23) + lambda_init
    la = acc_ref[0, dv:dv + 1, :]
    lb = acc_ref[1, dv:dv + 1, :]
    o = acc_ref[0, 0:dv, :] * (1.0 / la) - lam * (acc_ref[1, 0:dv, :] * (1.0 / lb))
    ms = jnp.mean(o * o, axis=0, keepdims=True)
    y = o * lax.rsqrt(ms + LN_EPS) * g_ref[...] * (1.0 - lambda_init)
    o_ref[...] = y.T.astype(BF16)


def _diff_attention(lam_p, subln_g, dqt, dk, dvt, lambda_init, seq):
    tq = Q_TILE
    n_kv, tk = dvt.shape[1], dvt.shape[3]
    assert n_kv % KV_UNROLL == 0 and KV_UNROLL % 2 == 0
    return pl.pallas_call(
        functools.partial(_diff_attn_kernel, lambda_init=lambda_init, tk=tk, n_kv=n_kv,
                          unroll=KV_UNROLL),
        grid=(DIFF_HEADS, seq // tq),
        in_specs=[_const_spec(lam_p.shape), _const_spec((DIFF_V_DIM, 1)),
                  pl.BlockSpec((None, LANES, tq), lambda h, i: (h, 0, i)),
                  pl.BlockSpec((None, seq, LANES), lambda h, i: (h, 0, 0)),
                  pl.BlockSpec((None, n_kv, LANES, tk), lambda h, i: (h, 0, 0, 0))],
        out_specs=pl.BlockSpec((tq, DIFF_V_DIM), lambda h, i: (i, h)),
        out_shape=jax.ShapeDtypeStruct((seq, DIFF_V), BF16),
        scratch_shapes=[pltpu.VMEM((2, DIFF_V_DIM + SUM_ROWS, tq), F32),
                        pltpu.VMEM((2, tk, tq), F32), pltpu.VMEM((2, tk, tq), F32),
                        pltpu.VMEM((2, LANES, tq), BF16)],
        compiler_params=_cparams(("arbitrary", "arbitrary")),
        name="diff_attn",
    )(lam_p, subln_g.reshape(DIFF_V_DIM, 1), dqt, dk, dvt)


def _window_attn_kernel(sink_ref, qt_ref, kp_ref, km_ref, kn_ref, vp_ref, vm_ref, vn_ref, o_ref,
                        *, n_blocks):
    i = pl.program_id(0)
    tq = qt_ref.shape[2]
    w = SWA_WINDOW
    kcat = jnp.concatenate([kp_ref[...], km_ref[...], kn_ref[...]], axis=0)
    vcat = jnp.concatenate([vp_ref[...], vm_ref[...], vn_ref[...]], axis=1)
    nk = tq + 2 * w
    r = lax.broadcasted_iota(jnp.int32, (nk, tq), 0)
    c = lax.broadcasted_iota(jnp.int32, (nk, tq), 1)
    r_lo = jnp.where(i > 0, 0, w)
    r_hi = jnp.where(i < n_blocks - 1, nk, tq + w)
    valid = (r >= jnp.maximum(c, r_lo)) & (r <= c + 2 * w) & (r < r_hi)
    sink = sink_ref[...] * LOG2E
    bias = jnp.where(valid, 0.0, NEG_BIG)
    group = SWA_Q_HEADS // SWA_KV_HEADS
    ones = jnp.ones((SUM_ROWS, nk), BF16)
    vts = [jnp.concatenate([vcat[c * HEAD_DIM:(c + 1) * HEAD_DIM, :], ones], axis=0)
           for c in range(SWA_KV_HEADS)]
    heads = [(c_kv * group + t, t, c_kv) for t in range(group) for c_kv in range(SWA_KV_HEADS)]
    outs = [None] * SWA_Q_HEADS

    def scores(t, c_kv):
        qx = _pair_masks(qt_ref[t])[c_kv]
        return jnp.dot(kcat, qx, preferred_element_type=F32) + bias

    def probs(s, head):
        sk = sink[:, head:head + 1]
        m = jnp.maximum(jnp.max(s, axis=0, keepdims=True), sk)
        return jnp.exp2(s - m).astype(BF16), jnp.exp2(sk - m)

    def pv(p, sink_term, c_kv):
        ot = jnp.dot(vts[c_kv], p, preferred_element_type=F32)
        denom = ot[HEAD_DIM:HEAD_DIM + 1] + sink_term
        return ot[:HEAD_DIM] * (1.0 / denom)

    n = len(heads)
    s_q, p_q = {}, {}
    for step in range(n + 2):
        if step < n:
            _, t, c_kv = heads[step]
            s_q[step] = scores(t, c_kv)
        if 1 <= step <= n:
            p_q[step - 1] = probs(s_q.pop(step - 1), heads[step - 1][0])
        if step >= 2:
            head, _, c_kv = heads[step - 2]
            outs[head] = pv(*p_q.pop(step - 2), c_kv)
    o_ref[...] = jnp.concatenate(outs, axis=0).T.astype(BF16)


def _window_attention(sink, sqt, sk, svt, seq):
    tq = Q_TILE
    w = SWA_WINDOW
    nb = seq // tq
    per = tq // w
    last = seq // w - 1
    prev_idx = lambda i: jnp.maximum(i * per - 1, 0)
    next_idx = lambda i: jnp.minimum((i + 1) * per, last)
    return pl.pallas_call(
        functools.partial(_window_attn_kernel, n_blocks=nb),
        grid=(nb,),
        in_specs=[_const_spec((1, SWA_Q_HEADS)),
                  pl.BlockSpec((SWA_Q // LANES, LANES, tq), lambda i: (0, 0, i)),
                  pl.BlockSpec((w, LANES), lambda i: (prev_idx(i), 0)),
                  pl.BlockSpec((tq, LANES), lambda i: (i, 0)),
                  pl.BlockSpec((w, LANES), lambda i: (next_idx(i), 0)),
                  pl.BlockSpec((LANES, w), lambda i: (0, prev_idx(i))),
                  pl.BlockSpec((LANES, tq), lambda i: (0, i)),
                  pl.BlockSpec((LANES, w), lambda i: (0, next_idx(i)))],
        out_specs=pl.BlockSpec((tq, SWA_Q), lambda i: (i, 0)),
        out_shape=jax.ShapeDtypeStruct((seq, SWA_Q), BF16),
        compiler_params=_cparams(("arbitrary",)),
        name="window_attn",
    )(sink.reshape(1, SWA_Q_HEADS), sqt, sk, sk, sk, svt, svt, svt)


def _shift_rows(x, prev_row, next_row):
    n = x.shape[0]
    row = lax.broadcasted_iota(jnp.int32, x.shape, 0)
    up = jnp.where(row == 0, prev_row, pltpu.roll(x, 1, axis=0))
    down = jnp.where(row == n - 1, next_row, pltpu.roll(x, n - 1, axis=0))
    return up, down


def _merge_kernel(h_ref, u_ref, up_ref, un_ref, ab_ref, yb_ref, yc_ref, cw_ref,
                  wg_ref, bg_ref, wb_ref, wo_ref, g_ref, b_ref, o_ref, *, n_tiles, alpha):
    i = pl.program_id(0)
    h = h_ref[...]
    hb = h.astype(BF16)
    u = u_ref[...].astype(F32)
    prev_row = jnp.where(i > 0, up_ref[SUBLANES - 1:SUBLANES, :].astype(F32), 0.0)
    next_row = jnp.where(i < n_tiles - 1, un_ref[0:1, :].astype(F32), 0.0)
    u_up, u_dn = _shift_rows(u, prev_row, next_row)
    cw = cw_ref[...]
    y_a = ab_ref[...].astype(F32) * (cw[0:1] * u_up + cw[1:2] * u + cw[2:3] * u_dn)
    ys = (y_a.astype(BF16), yb_ref[...], yc_ref[...])
    mix = None
    for n in range(N_BRANCH):
        lo, hi = n * D_MODEL, (n + 1) * D_MODEL
        gate = jax.nn.sigmoid(jnp.dot(hb, wg_ref[:, lo:hi], preferred_element_type=F32)
                              + bg_ref[:, lo:hi])
        term = gate * jnp.dot(ys[n], wb_ref[n], preferred_element_type=F32)
        mix = term if mix is None else mix + term
    out = jnp.dot(mix.astype(BF16), wo_ref[...], preferred_element_type=F32)
    o_ref[...] = _layer_norm(alpha * h + out, g_ref[...], b_ref[...])


def _halo_specs(tm, width, seq):
    per = tm // SUBLANES
    last = seq // SUBLANES - 1
    return (pl.BlockSpec((SUBLANES, width), lambda i: (jnp.maximum(i * per - 1, 0), 0)),
            pl.BlockSpec((SUBLANES, width), lambda i: (jnp.minimum((i + 1) * per, last), 0)))


def _merge(h, u, ab, y_b, y_c, conv_w, w_gate, b_gate, w_branch, w_o, ln_g, ln_b, alpha, seq):
    tm = ROW_TILE
    nt = seq // tm
    row_spec = lambda w: pl.BlockSpec((tm, w), lambda i: (i, 0))
    halo_prev, halo_next = _halo_specs(tm, CONV_WIDTH, seq)
    return pl.pallas_call(
        functools.partial(_merge_kernel, n_tiles=nt, alpha=alpha),
        grid=(nt,),
        in_specs=[row_spec(D_MODEL), row_spec(CONV_WIDTH), halo_prev, halo_next,
                  row_spec(CONV_WIDTH), row_spec(DIFF_V), row_spec(SWA_Q),
                  _const_spec((3, CONV_WIDTH)),
                  _const_spec((D_MODEL, N_BRANCH * D_MODEL)), _const_spec((1, N_BRANCH * D_MODEL)),
                  _const_spec((N_BRANCH, BRANCH_WIDTH, D_MODEL)), _const_spec((D_MODEL, D_MODEL)),
                  _const_spec((1, D_MODEL)), _const_spec((1, D_MODEL))],
        out_specs=row_spec(D_MODEL),
        out_shape=jax.ShapeDtypeStruct((seq, D_MODEL), F32),
        compiler_params=_cparams(("arbitrary",)),
        name="merge",
    )(h, u, u, u, ab, y_b, y_c, conv_w, w_gate, b_gate.reshape(1, -1), w_branch, w_o,
      ln_g.reshape(1, D_MODEL), ln_b.reshape(1, D_MODEL))


def _gelu_exact(x):
    return 0.5 * x * (1.0 + lax.erf(x * (2.0 ** -0.5)))


def _ffn_kernel(h_ref, hp_ref, hn_ref, wu_ref, cw_ref, wd_ref, g_ref, b_ref, o_ref, hx_ref,
                *, n_tiles, alpha):
    i = pl.program_id(0)
    tm = h_ref.shape[0]
    pad = SUBLANES
    h = h_ref[...]
    hx_ref[0:pad, :] = jnp.where(i > 0, hp_ref[...], 0.0).astype(BF16)
    hx_ref[pad:pad + tm, :] = h.astype(BF16)
    hx_ref[pad + tm:, :] = jnp.where(i < n_tiles - 1, hn_ref[...], 0.0).astype(BF16)
    hx = hx_ref[...]
    n = tm + 2 * pad

    def conv(x, w):
        up = pltpu.roll(x, 1, axis=0)
        dn = pltpu.roll(x, n - 1, axis=0)
        y = w[0:1] * up + w[1:2] * x + w[2:3] * dn
        return y[pad:pad + tm]

    gate = conv(jnp.dot(hx, wu_ref[:, :D_FF], preferred_element_type=F32), cw_ref[:, :D_FF])
    up = conv(jnp.dot(hx, wu_ref[:, D_FF:], preferred_element_type=F32), cw_ref[:, D_FF:])
    act = (_gelu_exact(gate) * up).astype(BF16)
    f = jnp.dot(act, wd_ref[...], preferred_element_type=F32)
    o_ref[...] = _layer_norm(alpha * h + f, g_ref[...], b_ref[...])


def _ffn(h, w_up, conv_w, w_down, ln_g, ln_b, alpha, seq):
    tm = ROW_TILE
    nt = seq // tm
    row_spec = pl.BlockSpec((tm, D_MODEL), lambda i: (i, 0))
    halo_prev, halo_next = _halo_specs(tm, D_MODEL, seq)
    return pl.pallas_call(
        functools.partial(_ffn_kernel, n_tiles=nt, alpha=alpha),
        grid=(nt,),
        in_specs=[row_spec, halo_prev, halo_next,
                  _const_spec((D_MODEL, 2 * D_FF)), _const_spec((3, 2 * D_FF)),
                  _const_spec((D_FF, D_MODEL)),
                  _const_spec((1, D_MODEL)), _const_spec((1, D_MODEL))],
        out_specs=row_spec,
        out_shape=jax.ShapeDtypeStruct((seq, D_MODEL), F32),
        scratch_shapes=[pltpu.VMEM((tm + 2 * SUBLANES, D_MODEL), BF16)],
        compiler_params=_cparams(("arbitrary",)),
        name="conv_ffn",
    )(h, h, h, w_up, conv_w, w_down, ln_g.reshape(1, D_MODEL), ln_b.reshape(1, D_MODEL))


def _split_in_proj(w):
    d = w.shape[0]
    cw = CONV_WIDTH
    o = 3 * cw
    dq = w[:, o:o + DIFF_QK]; o += DIFF_QK
    dk = w[:, o:o + DIFF_QK]; o += DIFF_QK
    dv = w[:, o:o + DIFF_V]; o += DIFF_V
    sq = w[:, o:o + SWA_Q]; o += SWA_Q
    sk = w[:, o:o + SWA_KV]; o += SWA_KV
    sv = w[:, o:o + SWA_KV]

    def pair_perm(x):
        return x.reshape(d, DIFF_HEADS, 2, 2, HALF).transpose(0, 1, 3, 2, 4).reshape(d, DIFF_QK)

    group = SWA_Q_HEADS // SWA_KV_HEADS
    sq_p = sq.reshape(d, SWA_KV_HEADS, group, 2, HALF).transpose(0, 2, 3, 1, 4).reshape(d, SWA_Q)
    sk_p = sk.reshape(d, SWA_KV_HEADS, 2, HALF).transpose(0, 2, 1, 3).reshape(d, SWA_KV)
    w_nat = jnp.concatenate([w[:, :3 * cw], pair_perm(dk), sk_p], axis=1)
    w_tr = jnp.concatenate([pair_perm(dq), dv, sq_p, sv], axis=1).T
    return w_nat, w_tr


def kernel(x, positions, ln_in_g, ln_in_b, w_in, conv_w, diff_lambda, diff_subln_g, swa_sink,
           w_branch_gate, b_branch_gate, w_branch, w_o, ln_mix_g, ln_mix_b,
           w_ffn_up, ffn_conv_w, w_ffn_down, ln_ffn_g, ln_ffn_b):
    batch, seq, _ = x.shape
    assert batch == 1 and seq % ROW_TILE == 0 and seq % Q_TILE == 0
    depth = w_in.shape[0]
    alpha = (2 * depth) ** 0.25
    tables = _rope_tables(positions, seq)
    h = x.reshape(seq, D_MODEL)
    for l in range(depth):
        lambda_init = 0.8 - 0.6 * math.exp(-0.3 * l)
        w_nat, w_tr = _split_in_proj(w_in[l].astype(BF16))
        if l == 0:
            h, *proj = _project(h, (ln_in_g, ln_in_b), w_nat, w_tr, tables, seq)
        else:
            proj = _project(h, None, w_nat, w_tr, tables, seq)
        u, ab, dk, sk, dqt, dvt, sqt, svt = proj
        y_b = _diff_attention(diff_lambda[l], diff_subln_g[l], dqt, dk, dvt, lambda_init, seq)
        y_c = _window_attention(swa_sink[l], sqt, sk, svt, seq)
        h = _merge(h, u, ab, y_b, y_c, conv_w[l], w_branch_gate[l].astype(BF16),
                   b_branch_gate[l], w_branch[l].astype(BF16), w_o[l].astype(BF16),
                   ln_mix_g[l], ln_mix_b[l], alpha, seq)
        h = _ffn(h, w_ffn_up[l].astype(BF16), ffn_conv_w[l], w_ffn_down[l].astype(BF16),
                 ln_ffn_g[l], ln_ffn_b[l], alpha, seq)
    return h.reshape(batch, seq, D_MODEL)
```

```python
import functools
import math

import jax
import jax.numpy as jnp
from jax import lax
from jax.experimental import pallas as pl
from jax.experimental.pallas import tpu as pltpu

F32 = jnp.float32
BF16 = jnp.bfloat16

D_MODEL = 1024
HEAD_DIM = 64
HALF = HEAD_DIM // 2
ROPE_THETA = 10000.0
CONV_WIDTH = 512
DIFF_HEADS = 4
DIFF_V_DIM = 2 * HEAD_DIM
SWA_Q_HEADS = 8
SWA_KV_HEADS = 2
SWA_WINDOW = 128
N_BRANCH = 3
BRANCH_WIDTH = 512
D_FF = 2816
LN_EPS = 1e-5
LANES = 128
SUBLANES = 8

DIFF_QK = DIFF_HEADS * 2 * HEAD_DIM
DIFF_V = DIFF_HEADS * DIFF_V_DIM
SWA_Q = SWA_Q_HEADS * HEAD_DIM
SWA_KV = SWA_KV_HEADS * HEAD_DIM
N_NAT = 3 * CONV_WIDTH + DIFF_QK + SWA_KV
N_TR = DIFF_QK + DIFF_V + SWA_Q + SWA_KV

ROW_TILE = 512
Q_TILE = 256
KV_UNROLL = 16
SUM_ROWS = 16
VMEM_LIMIT = 56 * 1024 * 1024
NEG_BIG = -1e30

LOG2E = math.log2(math.e)
Q_SCALE = HEAD_DIM ** -0.5 * LOG2E


def _cparams(sem):
    return pltpu.CompilerParams(dimension_semantics=sem, vmem_limit_bytes=VMEM_LIMIT)


def _const_spec(shape):
    nd = len(shape)
    return pl.BlockSpec(shape, lambda *_: (0,) * nd, pipeline_mode=pl.Buffered(1))


def _layer_norm(x, g, b):
    mu = jnp.mean(x, axis=-1, keepdims=True)
    xc = x - mu
    var = jnp.mean(xc * xc, axis=-1, keepdims=True)
    return xc * lax.rsqrt(var + LN_EPS) * g + b


def _rope_table_kernel(pos_ref, inv_ref, cos_n_ref, sin_n_ref, cos_t_ref, sin_t_ref):
    ang = inv_ref[...] * pos_ref[...].astype(F32)
    c, s = jnp.cos(ang), jnp.sin(ang)
    cos_t = jnp.concatenate([c, c, c, c], axis=0)
    sin_t = jnp.concatenate([-s, -s, s, s], axis=0)
    cos_t_ref[...] = cos_t
    sin_t_ref[...] = sin_t
    cos_n_ref[...] = cos_t.T
    sin_n_ref[...] = sin_t.T


def _rope_tables(positions, seq):
    ts = ROW_TILE
    inv_freq = 1.0 / (ROPE_THETA ** (jnp.arange(0, HEAD_DIM, 2, dtype=F32) / HEAD_DIM))
    return pl.pallas_call(
        _rope_table_kernel,
        grid=(seq // ts,),
        in_specs=[pl.BlockSpec((1, ts), lambda i: (0, i)), _const_spec((HALF, 1))],
        out_specs=[pl.BlockSpec((ts, LANES), lambda i: (i, 0)),
                   pl.BlockSpec((ts, LANES), lambda i: (i, 0)),
                   pl.BlockSpec((LANES, ts), lambda i: (0, i)),
                   pl.BlockSpec((LANES, ts), lambda i: (0, i))],
        out_shape=[jax.ShapeDtypeStruct((seq, LANES), F32),
                   jax.ShapeDtypeStruct((seq, LANES), F32),
                   jax.ShapeDtypeStruct((LANES, seq), F32),
                   jax.ShapeDtypeStruct((LANES, seq), F32)],
        compiler_params=_cparams(("arbitrary",)),
        name="rope_tables",
    )(positions.reshape(1, seq), inv_freq.reshape(HALF, 1))


def _rope_rows(x, cos_n, sin_n):
    return x * cos_n + pltpu.roll(x, 2 * HALF, axis=1) * sin_n


def _rope_cols(x, cos_t, sin_t):
    swapped = jnp.concatenate([x[2 * HALF:], x[:2 * HALF]], axis=0)
    return x * cos_t + swapped * sin_t


def _proj_kernel(*refs, apply_ln):
    if apply_ln:
        x_ref, g_ref, b_ref = refs[:3]
        refs = refs[3:]
        h = _layer_norm(x_ref[...], g_ref[...], b_ref[...])
    else:
        x_ref = refs[0]
        refs = refs[1:]
        h = x_ref[...]
    (wn_ref, wt_ref, cos_n_ref, sin_n_ref, cos_t_ref, sin_t_ref) = refs[:6]
    outs = refs[6:]
    if apply_ln:
        h_out_ref = outs[0]
        outs = outs[1:]
        h_out_ref[...] = h
    u_ref, ab_ref, dk_ref, sk_ref, dqt_ref, dvt_ref, sqt_ref, svt_ref = outs

    hb = h.astype(BF16)
    cos_n, sin_n = cos_n_ref[...], sin_n_ref[...]
    cos_t, sin_t = cos_t_ref[...], sin_t_ref[...]

    def nat(lo, hi):
        return jnp.dot(hb, wn_ref[:, lo:hi], preferred_element_type=F32)

    tr_all = lax.dot_general(wt_ref[...], hb, (((1,), (1,)), ((), ())),
                             preferred_element_type=F32)

    def tr(lo, hi):
        return tr_all[lo:hi]

    cw = CONV_WIDTH
    a_x = nat(0, cw)
    a_c = nat(2 * cw, 3 * cw)
    u_ref[...] = (a_c * a_x).astype(BF16)
    ab_ref[...] = nat(cw, 2 * cw).astype(BF16)

    base = 3 * cw
    for hh in range(DIFF_HEADS):
        xk = nat(base + hh * LANES, base + (hh + 1) * LANES)
        dk_ref[hh] = _rope_rows(xk, cos_n, sin_n).astype(BF16)
    base += DIFF_QK
    sk_ref[...] = _rope_rows(nat(base, base + SWA_KV), cos_n, sin_n).astype(BF16)

    for hh in range(DIFF_HEADS):
        xq = tr(hh * LANES, (hh + 1) * LANES)
        dqt_ref[hh] = (_rope_cols(xq, cos_t, sin_t) * Q_SCALE).astype(BF16)
    base = DIFF_QK
    for hh in range(DIFF_HEADS):
        dvt_ref[hh, 0] = tr(base + hh * LANES, base + (hh + 1) * LANES).astype(BF16)
    base += DIFF_V
    for t in range(SWA_Q // LANES):
        xq = tr(base + t * LANES, base + (t + 1) * LANES)
        sqt_ref[t] = (_rope_cols(xq, cos_t, sin_t) * Q_SCALE).astype(BF16)
    base += SWA_Q
    svt_ref[...] = tr(base, base + SWA_KV).astype(BF16)


def _project(h_or_x, ln_gb, w_nat, w_tr, tables, seq):
    tm = ROW_TILE
    nt = seq // tm
    apply_ln = ln_gb is not None
    cos_n, sin_n, cos_t, sin_t = tables
    row_spec = lambda w: pl.BlockSpec((tm, w), lambda i: (i, 0))
    in_specs = [row_spec(D_MODEL)]
    args = [h_or_x]
    if apply_ln:
        in_specs += [_const_spec((1, D_MODEL)), _const_spec((1, D_MODEL))]
        args += [ln_gb[0].reshape(1, D_MODEL), ln_gb[1].reshape(1, D_MODEL)]
    in_specs += [_const_spec((D_MODEL, N_NAT)), _const_spec((N_TR, D_MODEL)),
                 row_spec(LANES), row_spec(LANES),
                 pl.BlockSpec((LANES, tm), lambda i: (0, i)),
                 pl.BlockSpec((LANES, tm), lambda i: (0, i))]
    args += [w_nat, w_tr, cos_n, sin_n, cos_t, sin_t]
    out_specs, out_shape = [], []
    if apply_ln:
        out_specs.append(row_spec(D_MODEL))
        out_shape.append(jax.ShapeDtypeStruct((seq, D_MODEL), F32))
    out_specs += [
        row_spec(CONV_WIDTH), row_spec(CONV_WIDTH),
        pl.BlockSpec((DIFF_HEADS, tm, LANES), lambda i: (0, i, 0)),
        row_spec(LANES),
        pl.BlockSpec((DIFF_HEADS, LANES, tm), lambda i: (0, 0, i)),
        pl.BlockSpec((DIFF_HEADS, 1, LANES, tm), lambda i: (0, i, 0, 0)),
        pl.BlockSpec((SWA_Q // LANES, LANES, tm), lambda i: (0, 0, i)),
        pl.BlockSpec((LANES, tm), lambda i: (0, i)),
    ]
    out_shape += [
        jax.ShapeDtypeStruct((seq, CONV_WIDTH), BF16),
        jax.ShapeDtypeStruct((seq, CONV_WIDTH), BF16),
        jax.ShapeDtypeStruct((DIFF_HEADS, seq, LANES), BF16),
        jax.ShapeDtypeStruct((seq, LANES), BF16),
        jax.ShapeDtypeStruct((DIFF_HEADS, LANES, seq), BF16),
        jax.ShapeDtypeStruct((DIFF_HEADS, nt, LANES, tm), BF16),
        jax.ShapeDtypeStruct((SWA_Q // LANES, LANES, seq), BF16),
        jax.ShapeDtypeStruct((LANES, seq), BF16),
    ]
    return pl.pallas_call(
        functools.partial(_proj_kernel, apply_ln=apply_ln),
        grid=(nt,), in_specs=in_specs, out_specs=out_specs, out_shape=out_shape,
        compiler_params=_cparams(("arbitrary",)),
        name="proj_ln" if apply_ln else "proj",
    )(*args)


def _pair_masks(qt):
    row = lax.broadcasted_iota(jnp.int32, qt.shape, 0)
    is_a = (row & HALF) == 0
    zero = jnp.zeros_like(qt)
    return jnp.where(is_a, qt, zero), jnp.where(is_a, zero, qt)


def _diff_attn_kernel(lam_ref, g_ref, qt_ref, k_ref, vt_ref, o_ref, acc_ref, s0_ref, s1_ref, q_ref,
                      *, lambda_init, tk, n_kv, unroll):
    qa, qb = _pair_masks(qt_ref[...])
    q_ref[0] = qa
    q_ref[1] = qb
    tq = qa.shape[1]
    dv = DIFF_V_DIM
    acc_ref[...] = jnp.zeros_like(acc_ref)
    ones = jnp.ones((SUM_ROWS, tk), BF16)
    s_bufs = (s0_ref, s1_ref)

    def scores(j, s_ref):
        off = pl.multiple_of(j * tk, tk)
        kb = k_ref[pl.ds(off, tk), :]
        cms = []
        for x in range(2):
            s = jnp.dot(kb, q_ref[x], preferred_element_type=F32)
            s_ref[x] = s
            cms.append(jnp.max(s, axis=0, keepdims=True))
        return tuple(cms)

    def consume(j, s_ref, cm, m):
        vb = jnp.concatenate([vt_ref[j], ones], axis=0)
        m_out = []
        for x in range(2):
            m_new = jnp.maximum(m[x], cm[x])
            alpha = jnp.exp2(m[x] - m_new)
            p = jnp.exp2(s_ref[x] - m_new).astype(BF16)
            m_out.append(m_new)
            acc_ref[x] = alpha * acc_ref[x] + jnp.dot(vb, p, preferred_element_type=F32)
        return tuple(m_out)

    def body(t, carry):
        cm, m = carry
        j = unroll * t
        for u in range(unroll):
            cm_next = scores(jnp.minimum(j + u + 1, n_kv - 1), s_bufs[(u + 1) % 2])
            m = consume(j + u, s_bufs[u % 2], cm, m)
            cm = cm_next
        return cm, m

    m0 = jnp.full((1, tq), NEG_BIG, F32)
    lax.fori_loop(0, n_kv // unroll, body, (scores(0, s0_ref), (m0, m0)))

    lp = lam_ref[...]
    s01 = jnp.sum(lp[0:1] * lp[1:2], axis=1, keepdims=True)
    s23 = jnp.sum(lp[2:3] * lp[3:4], axis=1, keepdims=True)
    lam = jnp.exp(s01) - jnp.exp(s23) + lambda_init
    la = acc_ref[0, dv:dv + 1, :]
    lb = acc_ref[1, dv:dv + 1, :]
    o = acc_ref[0, 0:dv, :] * (1.0 / la) - lam * (acc_ref[1, 0:dv, :] * (1.0 / lb))
    ms = jnp.mean(o * o, axis=0, keepdims=True)
    y = o * lax.rsqrt(ms + LN_EPS) * g_ref[...] * (1.0 - lambda_init)
    o_ref[...] = y.T.astype(BF16)


def _diff_attention(lam_p, subln_g, dqt, dk, dvt, lambda_init, seq):
    tq = Q_TILE
    n_kv, tk = dvt.shape[1], dvt.shape[3]
    assert n_kv % KV_UNROLL == 0 and KV_UNROLL % 2 == 0
    return pl.pallas_call(
        functools.partial(_diff_attn_kernel, lambda_init=lambda_init, tk=tk, n_kv=n_kv,
                          unroll=KV_UNROLL),
        grid=(DIFF_HEADS, seq // tq),
        in_specs=[_const_spec(lam_p.shape), _const_spec((DIFF_V_DIM, 1)),
                  pl.BlockSpec((None, LANES, tq), lambda h, i: (h, 0, i)),
                  pl.BlockSpec((None, seq, LANES), lambda h, i: (h, 0, 0)),
                  pl.BlockSpec((None, n_kv, LANES, tk), lambda h, i: (h, 0, 0, 0))],
        out_specs=pl.BlockSpec((tq, DIFF_V_DIM), lambda h, i: (i, h)),
        out_shape=jax.ShapeDtypeStruct((seq, DIFF_V), BF16),
        scratch_shapes=[pltpu.VMEM((2, DIFF_V_DIM + SUM_ROWS, tq), F32),
                        pltpu.VMEM((2, tk, tq), F32), pltpu.VMEM((2, tk, tq), F32),
                        pltpu.VMEM((2, LANES, tq), BF16)],
        compiler_params=_cparams(("arbitrary", "arbitrary")),
        name="diff_attn",
    )(lam_p, subln_g.reshape(DIFF_V_DIM, 1), dqt, dk, dvt)


def _window_attn_kernel(sink_ref, qt_ref, kp_ref, km_ref, kn_ref, vp_ref, vm_ref, vn_ref, o_ref,
                        *, n_blocks):
    i = pl.program_id(0)
    tq = qt_ref.shape[2]
    w = SWA_WINDOW
    kcat = jnp.concatenate([kp_ref[...], km_ref[...], kn_ref[...]], axis=0)
    vcat = jnp.concatenate([vp_ref[...], vm_ref[...], vn_ref[...]], axis=1)
    nk = tq + 2 * w
    r = lax.broadcasted_iota(jnp.int32, (nk, tq), 0)
    c = lax.broadcasted_iota(jnp.int32, (nk, tq), 1)
    r_lo = jnp.where(i > 0, 0, w)
    r_hi = jnp.where(i < n_blocks - 1, nk, tq + w)
    valid = (r >= jnp.maximum(c, r_lo)) & (r <= c + 2 * w) & (r < r_hi)
    sink = sink_ref[...] * LOG2E
    bias = jnp.where(valid, 0.0, NEG_BIG)
    group = SWA_Q_HEADS // SWA_KV_HEADS
    ones = jnp.ones((SUM_ROWS, nk), BF16)
    vts = [jnp.concatenate([vcat[c * HEAD_DIM:(c + 1) * HEAD_DIM, :], ones], axis=0)
           for c in range(SWA_KV_HEADS)]
    heads = [(c_kv * group + t, t, c_kv) for t in range(group) for c_kv in range(SWA_KV_HEADS)]
    outs = [None] * SWA_Q_HEADS

    def scores(t, c_kv):
        qx = _pair_masks(qt_ref[t])[c_kv]
        return jnp.dot(kcat, qx, preferred_element_type=F32) + bias

    def probs(s, head):
        sk = sink[:, head:head + 1]
        m = jnp.maximum(jnp.max(s, axis=0, keepdims=True), sk)
        return jnp.exp2(s - m).astype(BF16), jnp.exp2(sk - m)

    def pv(p, sink_term, c_kv):
        ot = jnp.dot(vts[c_kv], p, preferred_element_type=F32)
        denom = ot[HEAD_DIM:HEAD_DIM + 1] + sink_term
        return ot[:HEAD_DIM] * (1.0 / denom)

    n = len(heads)
    s_q, p_q = {}, {}
    for step in range(n + 2):
        if step < n:
            _, t, c_kv = heads[step]
            s_q[step] = scores(t, c_kv)
        if 1 <= step <= n:
            p_q[step - 1] = probs(s_q.pop(step - 1), heads[step - 1][0])
        if step >= 2:
            head, _, c_kv = heads[step - 2]
            outs[head] = pv(*p_q.pop(step - 2), c_kv)
    o_ref[...] = jnp.concatenate(outs, axis=0).T.astype(BF16)


def _window_attention(sink, sqt, sk, svt, seq):
    tq = Q_TILE
    w = SWA_WINDOW
    nb = seq // tq
    per = tq // w
    last = seq // w - 1
    prev_idx = lambda i: jnp.maximum(i * per - 1, 0)
    next_idx = lambda i: jnp.minimum((i + 1) * per, last)
    return pl.pallas_call(
        functools.partial(_window_attn_kernel, n_blocks=nb),
        grid=(nb,),
        in_specs=[_const_spec((1, SWA_Q_HEADS)),
                  pl.BlockSpec((SWA_Q // LANES, LANES, tq), lambda i: (0, 0, i)),
                  pl.BlockSpec((w, LANES), lambda i: (prev_idx(i), 0)),
                  pl.BlockSpec((tq, LANES), lambda i: (i, 0)),
                  pl.BlockSpec((w, LANES), lambda i: (next_idx(i), 0)),
                  pl.BlockSpec((LANES, w), lambda i: (0, prev_idx(i))),
                  pl.BlockSpec((LANES, tq), lambda i: (0, i)),
                  pl.BlockSpec((LANES, w), lambda i: (0, next_idx(i)))],
        out_specs=pl.BlockSpec((tq, SWA_Q), lambda i: (i, 0)),
        out_shape=jax.ShapeDtypeStruct((seq, SWA_Q), BF16),
        compiler_params=_cparams(("arbitrary",)),
        name="window_attn",
    )(sink.reshape(1, SWA_Q_HEADS), sqt, sk, sk, sk, svt, svt, svt)


def _shift_rows(x, prev_row, next_row):
    n = x.shape[0]
    row = lax.broadcasted_iota(jnp.int32, x.shape, 0)
    up = jnp.where(row == 0, prev_row, pltpu.roll(x, 1, axis=0))
    down = jnp.where(row == n - 1, next_row, pltpu.roll(x, n - 1, axis=0))
    return up, down


def _merge_kernel(h_ref, u_ref, up_ref, un_ref, ab_ref, yb_ref, yc_ref, cw_ref,
                  wg_ref, bg_ref, wb_ref, wo_ref, g_ref, b_ref, o_ref, *, n_tiles, alpha):
    i = pl.program_id(0)
    h = h_ref[...]
    hb = h.astype(BF16)
    u = u_ref[...].astype(F32)
    prev_row = jnp.where(i > 0, up_ref[SUBLANES - 1:SUBLANES, :].astype(F32), 0.0)
    next_row = jnp.where(i < n_tiles - 1, un_ref[0:1, :].astype(F32), 0.0)
    u_up, u_dn = _shift_rows(u, prev_row, next_row)
    cw = cw_ref[...]
    y_a = ab_ref[...].astype(F32) * (cw[0:1] * u_up + cw[1:2] * u + cw[2:3] * u_dn)
    ys = (y_a.astype(BF16), yb_ref[...], yc_ref[...])
    mix = None
    for n in range(N_BRANCH):
        lo, hi = n * D_MODEL, (n + 1) * D_MODEL
        gate = jax.nn.sigmoid(jnp.dot(hb, wg_ref[:, lo:hi], preferred_element_type=F32)
                              + bg_ref[:, lo:hi])
        term = gate * jnp.dot(ys[n], wb_ref[n], preferred_element_type=F32)
        mix = term if mix is None else mix + term
    out = jnp.dot(mix.astype(BF16), wo_ref[...], preferred_element_type=F32)
    o_ref[...] = _layer_norm(alpha * h + out, g_ref[...], b_ref[...])


def _halo_specs(tm, width, seq):
    per = tm // SUBLANES
    last = seq // SUBLANES - 1
    return (pl.BlockSpec((SUBLANES, width), lambda i: (jnp.maximum(i * per - 1, 0), 0)),
            pl.BlockSpec((SUBLANES, width), lambda i: (jnp.minimum((i + 1) * per, last), 0)))


def _merge(h, u, ab, y_b, y_c, conv_w, w_gate, b_gate, w_branch, w_o, ln_g, ln_b, alpha, seq):
    tm = ROW_TILE
    nt = seq // tm
    row_spec = lambda w: pl.BlockSpec((tm, w), lambda i: (i, 0))
    halo_prev, halo_next = _halo_specs(tm, CONV_WIDTH, seq)
    return pl.pallas_call(
        functools.partial(_merge_kernel, n_tiles=nt, alpha=alpha),
        grid=(nt,),
        in_specs=[row_spec(D_MODEL), row_spec(CONV_WIDTH), halo_prev, halo_next,
                  row_spec(CONV_WIDTH), row_spec(DIFF_V), row_spec(SWA_Q),
                  _const_spec((3, CONV_WIDTH)),
                  _const_spec((D_MODEL, N_BRANCH * D_MODEL)), _const_spec((1, N_BRANCH * D_MODEL)),
                  _const_spec((N_BRANCH, BRANCH_WIDTH, D_MODEL)), _const_spec((D_MODEL, D_MODEL)),
                  _const_spec((1, D_MODEL)), _const_spec((1, D_MODEL))],
        out_specs=row_spec(D_MODEL),
        out_shape=jax.ShapeDtypeStruct((seq, D_MODEL), F32),
        compiler_params=_cparams(("arbitrary",)),
        name="merge",
    )(h, u, u, u, ab, y_b, y_c, conv_w, w_gate, b_gate.reshape(1, -1), w_branch, w_o,
      ln_g.reshape(1, D_MODEL), ln_b.reshape(1, D_MODEL))


def _gelu_exact(x):
    return 0.5 * x * (1.0 + lax.erf(x * (2.0 ** -0.5)))


def _ffn_kernel(h_ref, hp_ref, hn_ref, wu_ref, cw_ref, wd_ref, g_ref, b_ref, o_ref, hx_ref,
                *, n_tiles, alpha):
    i = pl.program_id(0)
    tm = h_ref.shape[0]
    pad = SUBLANES
    h = h_ref[...]
    hx_ref[0:pad, :] = jnp.where(i > 0, hp_ref[...], 0.0).astype(BF16)
    hx_ref[pad:pad + tm, :] = h.astype(BF16)
    hx_ref[pad + tm:, :] = jnp.where(i < n_tiles - 1, hn_ref[...], 0.0).astype(BF16)
    hx = hx_ref[...]
    n = tm + 2 * pad

    def conv(x, w):
        up = pltpu.roll(x, 1, axis=0)
        dn = pltpu.roll(x, n - 1, axis=0)
        y = w[0:1] * up + w[1:2] * x + w[2:3] * dn
        return y[pad:pad + tm]

    gate = conv(jnp.dot(hx, wu_ref[:, :D_FF], preferred_element_type=F32), cw_ref[:, :D_FF])
    up = conv(jnp.dot(hx, wu_ref[:, D_FF:], preferred_element_type=F32), cw_ref[:, D_FF:])
    act = (_gelu_exact(gate) * up).astype(BF16)
    f = jnp.dot(act, wd_ref[...], preferred_element_type=F32)
    o_ref[...] = _layer_norm(alpha * h + f, g_ref[...], b_ref[...])


def _ffn(h, w_up, conv_w, w_down, ln_g, ln_b, alpha, seq):
    tm = ROW_TILE
    nt = seq // tm
    row_spec = pl.BlockSpec((tm, D_MODEL), lambda i: (i, 0))
    halo_prev, halo_next = _halo_specs(tm, D_MODEL, seq)
    return pl.pallas_call(
        functools.partial(_ffn_kernel, n_tiles=nt, alpha=alpha),
        grid=(nt,),
        in_specs=[row_spec, halo_prev, halo_next,
                  _const_spec((D_MODEL, 2 * D_FF)), _const_spec((3, 2 * D_FF)),
                  _const_spec((D_FF, D_MODEL)),
                  _const_spec((1, D_MODEL)), _const_spec((1, D_MODEL))],
        out_specs=row_spec,
        out_shape=jax.ShapeDtypeStruct((seq, D_MODEL), F32),
        scratch_shapes=[pltpu.VMEM((tm + 2 * SUBLANES, D_MODEL), BF16)],
        compiler_params=_cparams(("arbitrary",)),
        name="conv_ffn",
    )(h, h, h, w_up, conv_w, w_down, ln_g.reshape(1, D_MODEL), ln_b.reshape(1, D_MODEL))


def _split_in_proj(w):
    d = w.shape[0]
    cw = CONV_WIDTH
    o = 3 * cw
    dq = w[:, o:o + DIFF_QK]; o += DIFF_QK
    dk = w[:, o:o + DIFF_QK]; o += DIFF_QK
    dv = w[:, o:o + DIFF_V]; o += DIFF_V
    sq = w[:, o:o + SWA_Q]; o += SWA_Q
    sk = w[:, o:o + SWA_KV]; o += SWA_KV
    sv = w[:, o:o + SWA_KV]

    def pair_perm(x):
        return x.reshape(d, DIFF_HEADS, 2, 2, HALF).transpose(0, 1, 3, 2, 4).reshape(d, DIFF_QK)

    group = SWA_Q_HEADS // SWA_KV_HEADS
    sq_p = sq.reshape(d, SWA_KV_HEADS, group, 2, HALF).transpose(0, 2, 3, 1, 4).reshape(d, SWA_Q)
    sk_p = sk.reshape(d, SWA_KV_HEADS, 2, HALF).transpose(0, 2, 1, 3).reshape(d, SWA_KV)
    w_nat = jnp.concatenate([w[:, :3 * cw], pair_perm(dk), sk_p], axis=1)
    w_tr = jnp.concatenate([pair_perm(dq), dv, sq_p, sv], axis=1).T
    return w_nat, w_tr


def kernel(x, positions, ln_in_g, ln_in_b, w_in, conv_w, diff_lambda, diff_subln_g, swa_sink,
           w_branch_gate, b_branch_gate, w_branch, w_o, ln_mix_g, ln_mix_b,
           w_ffn_up, ffn_conv_w, w_ffn_down, ln_ffn_g, ln_ffn_b):
    batch, seq, _ = x.shape
    assert batch == 1 and seq % ROW_TILE == 0 and seq % Q_TILE == 0
    depth = w_in.shape[0]
    alpha = (2 * depth) ** 0.25
    tables = _rope_tables(positions, seq)
    h = x.reshape(seq, D_MODEL)
    for l in range(depth):
        lambda_init = 0.8 - 0.6 * math.exp(-0.3 * l)
        w_nat, w_tr = _split_in_proj(w_in[l].astype(BF16))
        if l == 0:
            h, *proj = _project(h, (ln_in_g, ln_in_b), w_nat, w_tr, tables, seq)
        else:
            proj = _project(h, None, w_nat, w_tr, tables, seq)
        u, ab, dk, sk, dqt, dvt, sqt, svt = proj
        y_b = _diff_attention(diff_lambda[l], diff_subln_g[l], dqt, dk, dvt, lambda_init, seq)
        y_c = _window_attention(swa_sink[l], sqt, sk, svt, seq)
        h = _merge(h, u, ab, y_b, y_c, conv_w[l], w_branch_gate[l].astype(BF16),
                   b_branch_gate[l], w_branch[l].astype(BF16), w_o[l].astype(BF16),
                   ln_mix_g[l], ln_mix_b[l], alpha, seq)
        h = _ffn(h, w_ffn_up[l].astype(BF16), ffn_conv_w[l], w_ffn_down[l].astype(BF16),
                 ln_ffn_g[l], ln_ffn_b[l], alpha, seq)
    return h.reshape(batch, seq, D_MODEL)
```
